```python
import jax
import jax.numpy as jnp
from jax import lax
import numpy as np

D_MODEL = 1024
BATCH = 4
SEQ = 4096
DEPTH = 2

D_FF = 2816
RWKV_HEADS = 8
RWKV_HEAD_DIM = 64
RWKV_WIDTH = RWKV_HEADS * RWKV_HEAD_DIM
RWKV_DECAY_RANK = 32
RWKV_A_RANK = 32
RWKV_GATE_RANK = 96
MLSTM_HEADS = 4
MLSTM_HEAD_DIM = 128
MLSTM_WIDTH = MLSTM_HEADS * MLSTM_HEAD_DIM
MLSTM_CONV = 4
MLSTM_CHUNK = 128
RWKV_COLS = 3 * RWKV_WIDTH + RWKV_DECAY_RANK + RWKV_A_RANK + RWKV_GATE_RANK
MLSTM_COLS = 4 * MLSTM_WIDTH + 2 * MLSTM_HEADS
GATE_COLS = 2 * D_MODEL
N_IN = RWKV_COLS + MLSTM_COLS + GATE_COLS
RMS_EPS = 1e-6
RWKV_GN_EPS = 64e-5
MLSTM_NORM_EPS = 1e-5
L2_EPS = 1e-12

kernel_name = 'hybrid_rwkv7_mlstm_macaron_adaln'


def rms_norm(x, g):
    xf = x.astype(jnp.float32)
    y = xf * lax.rsqrt(jnp.mean(xf * xf, axis=-1, keepdims=True) + RMS_EPS)
    return (y * g.astype(jnp.float32)).astype(x.dtype)


def modulate(h, shift, scale):
    return h * (1 + scale[:, None, :]) + shift[:, None, :]


def swiglu(h, w_in, w_out):
    gate, up = jnp.split(h @ w_in, 2, axis=-1)
    return (jax.nn.silu(gate) * up) @ w_out


def head_norm(y, eps):
    mu = jnp.mean(y, axis=-1, keepdims=True)
    yc = y - mu
    return yc * lax.rsqrt(jnp.mean(yc * yc, axis=-1, keepdims=True) + eps)


def causal_depthwise_conv(x, w, b):
    width, chans = w.shape
    y = lax.conv_general_dilated(x, w[:, None, :], window_strides=(1,), padding=[(width - 1, 0)],
                                 dimension_numbers=('NWC', 'WIO', 'NWC'), feature_group_count=chans)
    return y + b


def rwkv7_recurrence(r, w, k, v, a, b):
    bsz, _, heads, n = r.shape

    def step(s, inp):
        r_t, w_t, k_t, v_t, a_t, b_t = inp
        sa = jnp.einsum('bhvk,bhk->bhv', s, a_t)
        s = s * w_t[:, :, None, :] + sa[..., None] * b_t[:, :, None, :] + v_t[..., None] * k_t[:, :, None, :]
        return s, jnp.einsum('bhvk,bhk->bhv', s, r_t)

    xs = tuple(jnp.moveaxis(t, 1, 0) for t in (r, w, k, v, a, b))
    s0 = jnp.zeros((bsz, heads, n, n), jnp.float32)
    _, y = lax.scan(step, s0, xs)
    return jnp.moveaxis(y, 0, 1)


def rwkv7_branch(p, mu, w0, w_up, a0, a_up, g_up, k_k, k_a, r_k, ln_g, ln_b):
    bsz, t_len, _ = p.shape
    hd = (bsz, t_len, RWKV_HEADS, RWKV_HEAD_DIM)
    pf = p.astype(jnp.float32)
    prev = jnp.pad(pf, ((0, 0), (1, 0), (0, 0)))[:, :-1]
    pf = pf + (prev - pf) * mu
    o1 = RWKV_WIDTH
    o4 = 3 * o1 + RWKV_DECAY_RANK
    o5 = o4 + RWKV_A_RANK
    r, k, v, w_lo, a_lo, g_lo = jnp.split(pf, [o1, 2 * o1, 3 * o1, o4, o5], axis=-1)
    w = w0 + jnp.tanh(w_lo) @ w_up
    w = -jax.nn.softplus(-w) - 0.5
    decay = jnp.exp(-jnp.exp(w))
    a = jax.nn.sigmoid(a0 + a_lo @ a_up)
    g = jax.nn.sigmoid(g_lo) @ g_up
    kk = (k * k_k).reshape(hd)
    kk = kk / jnp.maximum(jnp.sqrt(jnp.sum(kk * kk, axis=-1, keepdims=True)), L2_EPS)
    k = k * (1 + (a - 1) * k_a)
    r, k, v, decay, a = (t.reshape(hd) for t in (r, k, v, decay, a))
    y = rwkv7_recurrence(r, decay, k, v, -kk, kk * a)
    y = head_norm(y, RWKV_GN_EPS).reshape(bsz, t_len, RWKV_WIDTH) * ln_g + ln_b
    bonus = (jnp.sum(r * k * r_k, axis=-1, keepdims=True) * v).reshape(bsz, t_len, RWKV_WIDTH)
    return (y + bonus) * g


def mlstm_branch(p, conv_w, conv_b, gate_b, ln_g):
    bsz, t_len, _ = p.shape
    nh, dh, blk = MLSTM_HEADS, MLSTM_HEAD_DIM, MLSTM_CHUNK
    nc = t_len // blk
    wd = MLSTM_WIDTH
    pf = p.astype(jnp.float32)
    qk, v, o, gates = jnp.split(pf, [2 * wd, 3 * wd, 4 * wd], axis=-1)
    qk = jax.nn.silu(causal_depthwise_conv(qk, conv_w.astype(jnp.float32), conv_b))
    q, k = jnp.split(qk, 2, axis=-1)
    i_pre, f_pre = jnp.split(gates + gate_b, 2, axis=-1)

    def chunks(t):
        t = t.reshape((bsz, nc, blk, nh) + t.shape[3:])
        return jnp.moveaxis(t, 3, 1)

    q = chunks(q.reshape(bsz, t_len, nh, dh)) * dh ** -0.5
    k = chunks(k.reshape(bsz, t_len, nh, dh))
    v = chunks(v.reshape(bsz, t_len, nh, dh))
    log_i = chunks(i_pre)
    log_f = jax.nn.log_sigmoid(chunks(f_pre))
    b = jnp.cumsum(log_f, axis=-1)
    b_end = b[..., -1]
    causal = jnp.tril(jnp.ones((blk, blk), dtype=bool))
    d = jnp.where(causal, b[..., :, None] - b[..., None, :] + log_i[..., None, :], -jnp.inf)
    g_end = b_end[..., None] - b + log_i

    def step(carry, inp):
        c_st, n_st, m_st = carry
        k_c, v_c, g_c, be = inp
        m_new = jnp.maximum(be + m_st, jnp.max(g_c, axis=-1))
        keep = jnp.exp(be + m_st - m_new)
        wgt = jnp.exp(g_c - m_new[..., None])
        c_new = keep[..., None, None] * c_st + jnp.einsum('bhl,bhld,bhle->bhde', wgt, k_c, v_c)
        n_new = keep[..., None] * n_st + jnp.einsum('bhl,bhld->bhd', wgt, k_c)
        return (c_new, n_new, m_new), (c_st, n_st, m_st)

    init = (jnp.zeros((bsz, nh, dh, dh), jnp.float32), jnp.zeros((bsz, nh, dh), jnp.float32),
            jnp.full((bsz, nh), -jnp.inf, jnp.float32))
    xs = (jnp.moveaxis(k, 2, 0), jnp.moveaxis(v, 2, 0), jnp.moveaxis(g_end, 2, 0), jnp.moveaxis(b_end, 2, 0))
    _, (c_prev, n_prev, m_prev) = lax.scan(step, init, xs)
    c_prev = jnp.moveaxis(c_prev, 0, 2)
    n_prev = jnp.moveaxis(n_prev, 0, 2)
    m_prev = jnp.moveaxis(m_prev, 0, 2)
    m_inter = b + m_prev[..., None]
    m_t = jnp.maximum(m_inter, jnp.max(d, axis=-1))
    scores = jnp.einsum('bhcld,bhcsd->bhcls', q, k) * jnp.exp(d - m_t[..., None])
    inter = jnp.exp(m_inter - m_t)
    num = jnp.einsum('bhcls,bhcse->bhcle', scores, v) + inter[..., None] * jnp.einsum('bhcld,bhcde->bhcle', q, c_prev)
    den = jnp.sum(scores, axis=-1) + inter * jnp.einsum('bhcld,bhcd->bhcl', q, n_prev)
    h = num / jnp.maximum(jnp.abs(den), jnp.exp(-m_t))[..., None]
    h = jnp.moveaxis(h, 1, 3).reshape(bsz, t_len, nh, dh)
    h = head_norm(h, MLSTM_NORM_EPS).reshape(bsz, t_len, wd) * ln_g
    return jax.nn.sigmoid(o) * h


def token_mixer(h, w_in, rwkv_mu, rwkv_w0, rwkv_w_up, rwkv_a0, rwkv_a_up, rwkv_g_up, rwkv_k_k, rwkv_k_a,
                rwkv_r_k, rwkv_ln_g, rwkv_ln_b, mlstm_conv_w, mlstm_conv_b, mlstm_gate_b, mlstm_ln_g,
                branch_w_rwkv, branch_w_mlstm, w_out):
    p = h @ w_in
    p_r, p_m, p_g = jnp.split(p, [RWKV_COLS, RWKV_COLS + MLSTM_COLS], axis=-1)
    y_r = rwkv7_branch(p_r, rwkv_mu, rwkv_w0, rwkv_w_up, rwkv_a0, rwkv_a_up, rwkv_g_up, rwkv_k_k, rwkv_k_a,
                       rwkv_r_k, rwkv_ln_g, rwkv_ln_b) @ branch_w_rwkv
    y_m = mlstm_branch(p_m, mlstm_conv_w, mlstm_conv_b, mlstm_gate_b, mlstm_ln_g) @ branch_w_mlstm
    g_r, g_m = jnp.split(jax.nn.sigmoid(p_g.astype(jnp.float32)), 2, axis=-1)
    return ((g_r * y_r + g_m * y_m) @ w_out).astype(h.dtype)


def setup_inputs(seed: int = 0) -> dict:
    key = jax.random.key(seed)
    ks = jax.random.split(key, 32)
    f32 = jnp.float32

    def nrm(k, shape, scale):
        return jax.random.normal(k, shape, f32) * scale

    dm, w, mw = D_MODEL, RWKV_WIDTH, MLSTM_WIDTH
    return {
        'x': nrm(ks[0], (BATCH, SEQ, dm), 1.0),
        'c': nrm(ks[1], (BATCH, dm), 1.0),
        'ada_w': nrm(ks[2], (DEPTH, dm, 9 * dm), 0.1 * dm ** -0.5),
        'ada_b': nrm(ks[3], (DEPTH, 9 * dm), 0.02),
        'norm_g': 1.0 + nrm(ks[4], (DEPTH, 3, dm), 0.02),
        'ffn_w_in': nrm(ks[5], (DEPTH, 2, dm, 2 * D_FF), dm ** -0.5),
        'ffn_w_out': nrm(ks[6], (DEPTH, 2, D_FF, dm), D_FF ** -0.5),
        'mix_w_in': nrm(ks[7], (DEPTH, dm, N_IN), dm ** -0.5),
        'rwkv_mu': jax.random.uniform(ks[8], (DEPTH, RWKV_COLS), f32, 0.1, 0.9),
        'rwkv_w0': jnp.linspace(-6.0, -1.0, w, dtype=f32)[None, :] + nrm(ks[9], (DEPTH, w), 0.1),
        'rwkv_w_up': nrm(ks[10], (DEPTH, RWKV_DECAY_RANK, w), 0.1 * RWKV_DECAY_RANK ** -0.5),
        'rwkv_a0': nrm(ks[11], (DEPTH, w), 0.1),
        'rwkv_a_up': nrm(ks[12], (DEPTH, RWKV_A_RANK, w), 0.5 * RWKV_A_RANK ** -0.5),
        'rwkv_g_up': nrm(ks[13], (DEPTH, RWKV_GATE_RANK, w), RWKV_GATE_RANK ** -0.5),
        'rwkv_k_k': 0.85 + nrm(ks[14], (DEPTH, w), 0.02),
        'rwkv_k_a': 1.0 + nrm(ks[15], (DEPTH, w), 0.02),
        'rwkv_r_k': nrm(ks[16], (DEPTH, RWKV_HEADS, RWKV_HEAD_DIM), 0.1),
        'rwkv_ln_g': 1.0 + nrm(ks[17], (DEPTH, w), 0.02),
        'rwkv_ln_b': nrm(ks[18], (DEPTH, w), 0.02),
        'mlstm_conv_w': nrm(ks[19], (DEPTH, MLSTM_CONV, 2 * mw), MLSTM_CONV ** -0.5),
        'mlstm_conv_b': nrm(ks[20], (DEPTH, 2 * mw), 0.02),
        'mlstm_gate_b': jnp.concatenate([
            nrm(ks[21], (DEPTH, MLSTM_HEADS), 0.1),
            jnp.linspace(3.0, 6.0, MLSTM_HEADS, dtype=f32)[None, :] + nrm(ks[22], (DEPTH, MLSTM_HEADS), 0.1)],
            axis=-1),
        'mlstm_ln_g': 1.0 + nrm(ks[23], (DEPTH, mw), 0.02),
        'branch_w_rwkv': nrm(ks[24], (DEPTH, w, dm), w ** -0.5),
        'branch_w_mlstm': nrm(ks[25], (DEPTH, mw, dm), mw ** -0.5),
        'mix_w_out': nrm(ks[26], (DEPTH, dm, dm), dm ** -0.5),
        'final_g': 1.0 + nrm(ks[27], (dm,), 0.02),
    }


def reference(x, c, ada_w, ada_b, norm_g, ffn_w_in, ffn_w_out, mix_w_in, rwkv_mu, rwkv_w0, rwkv_w_up,
              rwkv_a0, rwkv_a_up, rwkv_g_up, rwkv_k_k, rwkv_k_a, rwkv_r_k, rwkv_ln_g, rwkv_ln_b,
              mlstm_conv_w, mlstm_conv_b, mlstm_gate_b, mlstm_ln_g, branch_w_rwkv, branch_w_mlstm,
              mix_w_out, final_g):
    cond = jax.nn.silu(c)
    for l in range(DEPTH):
        mod = cond @ ada_w[l] + ada_b[l]
        sh0, sc0, gt0, sh1, sc1, gt1, sh2, sc2, gt2 = jnp.split(mod, 9, axis=-1)
        h = modulate(rms_norm(x, norm_g[l, 0]), sh0, sc0)
        x = x + (0.5 * (1 + gt0))[:, None, :] * swiglu(h, ffn_w_in[l, 0], ffn_w_out[l, 0])
        h = modulate(rms_norm(x, norm_g[l, 1]), sh1, sc1)
        x = x + (1 + gt1)[:, None, :] * token_mixer(
            h, mix_w_in[l], rwkv_mu[l], rwkv_w0[l], rwkv_w_up[l], rwkv_a0[l], rwkv_a_up[l], rwkv_g_up[l],
            rwkv_k_k[l], rwkv_k_a[l], rwkv_r_k[l], rwkv_ln_g[l], rwkv_ln_b[l], mlstm_conv_w[l], mlstm_conv_b[l],
            mlstm_gate_b[l], mlstm_ln_g[l], branch_w_rwkv[l], branch_w_mlstm[l], mix_w_out[l])
        h = modulate(rms_norm(x, norm_g[l, 2]), sh2, sc2)
        x = x + (0.5 * (1 + gt2))[:, None, :] * swiglu(h, ffn_w_in[l, 1], ffn_w_out[l, 1])
    return rms_norm(x, final_g)
```

```python
import functools

import jax
import jax.numpy as jnp
from jax import lax
from jax.experimental import pallas as pl
from jax.experimental.pallas import tpu as pltpu

F32 = jnp.float32
BF16 = jnp.bfloat16

RWKV_HEADS = 8
RWKV_HEAD_DIM = 64
RWKV_DECAY_RANK = 32
RWKV_A_RANK = 32
RWKV_GATE_RANK = 96
MLSTM_HEADS = 4
MLSTM_HEAD_DIM = 128
MLSTM_CONV = 4
MLSTM_CHUNK = 128
RMS_EPS = 1e-6
RWKV_GN_EPS = 64e-5
MLSTM_NORM_EPS = 1e-5
L2_EPS = 1e-12

LANES = 128
SUBLANES = 8
VMEM_LIMIT = 56 * 1024 * 1024

RWKV_CHUNK = 64
RWKV_GROUP_HEADS = 4
RWKV_LOWRANK_PAD = 256


def _bdot(a, b):
    return jnp.dot(a.astype(BF16), b.astype(BF16), preferred_element_type=F32)


def _bdot_nt(a, b):
    return lax.dot_general(a.astype(BF16), b.astype(BF16), (((1,), (1,)), ((), ())),
                           preferred_element_type=F32)


def _bdot_tn(a, b):
    return lax.dot_general(a.astype(BF16), b.astype(BF16), (((0,), (0,)), ((), ())),
                           preferred_element_type=F32)


def _split_bf16(x, parts):
    out = []
    rem = x
    for _ in range(parts):
        p = rem.astype(BF16)
        out.append(p)
        rem = rem - p.astype(F32)
    return out


def _dot_exact_rhs(x, m_bf16, parts):
    acc = None
    for p in _split_bf16(x, parts):
        t = jnp.dot(p, m_bf16, preferred_element_type=F32)
        acc = t if acc is None else acc + t
    return acc


def _exact_lhs_dot(m_bf16, x, parts):
    acc = None
    for p in _split_bf16(x, parts):
        t = jnp.dot(m_bf16, p, preferred_element_type=F32)
        acc = t if acc is None else acc + t
    return acc


def _rms_mod(x, g, shift, scale):
    y = x * lax.rsqrt(jnp.mean(x * x, axis=-1, keepdims=True) + RMS_EPS) * g
    return y * (1.0 + scale) + shift


def _softplus(z):
    return jnp.maximum(z, 0.0) + jnp.log1p(jnp.exp(-jnp.abs(z)))


def _ada_kernel(c_ref, w_ref, b_ref, o_ref):
    c = c_ref[...]
    cond = c * jax.nn.sigmoid(c)
    o_ref[0] = _bdot(cond, w_ref[0]) + b_ref[0]


def _ada_call(c_pad, ada_w, ada_b):
    depth, dm, n9 = ada_w.shape
    tn = n9 // 4
    rows = c_pad.shape[0]
    return pl.pallas_call(
        _ada_kernel,
        grid=(depth, n9 // tn),
        in_specs=[
            pl.BlockSpec((rows, dm), lambda l, j: (0, 0)),
            pl.BlockSpec((1, dm, tn), lambda l, j: (l, 0, j)),
            pl.BlockSpec((1, 1, tn), lambda l, j: (l, 0, j)),
        ],
        out_specs=pl.BlockSpec((1, rows, tn), lambda l, j: (l, 0, j)),
        out_shape=jax.ShapeDtypeStruct((depth, rows, n9), F32),
        compiler_params=pltpu.CompilerParams(
            dimension_semantics=("arbitrary", "arbitrary"), vmem_limit_bytes=VMEM_LIMIT),
        name="ada_mod",
    )(c_pad, ada_w, ada_b.reshape(depth, 1, n9))


def _ffn_kernel(x_ref, mod_ref, g_ref, wg_ref, wu_ref, wo_ref, fg_ref, o_ref, xn_ref, acc_ref,
                *, mod_row, final_norm):
    j = pl.program_id(1)

    @pl.when(j == 0)
    def _():
        h = _rms_mod(x_ref[...], g_ref[...], mod_ref[0, mod_row:mod_row + 1, :],
                     mod_ref[0, mod_row + 1:mod_row + 2, :])
        xn_ref[...] = h.astype(BF16)
        acc_ref[...] = jnp.zeros_like(acc_ref)

    xn = xn_ref[...]
    gate = jnp.dot(xn, wg_ref[...], preferred_element_type=F32)
    up = jnp.dot(xn, wu_ref[...], preferred_element_type=F32)
    act = (gate * jax.nn.sigmoid(gate)) * up
    acc_ref[...] += jnp.dot(act.astype(BF16), wo_ref[...], preferred_element_type=F32)

    @pl.when(j == pl.num_programs(1) - 1)
    def _():
        gt = mod_ref[0, mod_row + 2:mod_row + 3, :]
        out = x_ref[...] + (0.5 * (1.0 + gt)) * acc_ref[...]
        if final_norm:
            out = out * lax.rsqrt(jnp.mean(out * out, axis=-1, keepdims=True) + RMS_EPS) * fg_ref[...]
        o_ref[...] = out


def _ffn_call(x2, mod_l, norm_g, w_in, w_out, final_g, *, seq, mod_row, final_norm, tm, tf):
    n, dm = x2.shape
    dff = w_out.shape[0]
    nf = dff // tf
    kern = functools.partial(_ffn_kernel, mod_row=mod_row, final_norm=final_norm)
    return pl.pallas_call(
        kern,
        grid=(n // tm, nf),
        in_specs=[
            pl.BlockSpec((tm, dm), lambda i, j: (i, 0)),
            pl.BlockSpec((1, 9, dm), lambda i, j: ((i * tm) // seq, 0, 0)),
            pl.BlockSpec((1, dm), lambda i, j: (0, 0)),
            pl.BlockSpec((dm, tf), lambda i, j: (0, j)),
            pl.BlockSpec((dm, tf), lambda i, j: (0, j + nf)),
            pl.BlockSpec((tf, dm), lambda i, j: (j, 0)),
            pl.BlockSpec((1, dm), lambda i, j: (0, 0)),
        ],
        out_specs=pl.BlockSpec((tm, dm), lambda i, j: (i, 0)),
        out_shape=jax.ShapeDtypeStruct((n, dm), F32),
        scratch_shapes=[pltpu.VMEM((tm, dm), BF16), pltpu.VMEM((tm, dm), F32)],
        compiler_params=pltpu.CompilerParams(
            dimension_semantics=("parallel", "arbitrary"), vmem_limit_bytes=VMEM_LIMIT),
        name="ffn",
    )(x2, mod_l, norm_g, w_in, w_in, w_out, final_g)


def _mixin_kernel(x_ref, mod_ref, g_ref, wr_ref, wm_ref, wg_ref, pr_ref, pm_ref, pg_ref):
    h = _rms_mod(x_ref[...], g_ref[...], mod_ref[0, 3:4, :], mod_ref[0, 4:5, :]).astype(BF16)
    pr_ref[...] = jnp.dot(h, wr_ref[...], preferred_element_type=F32)
    pm_ref[...] = jnp.dot(h, wm_ref[...], preferred_element_type=F32)
    pg_ref[...] = jnp.dot(h, wg_ref[...], preferred_element_type=F32)


def _mixin_call(x2, mod_l, norm_g, w_r, w_m, w_g, *, seq, tm):
    n, dm = x2.shape
    nr, nm, ng = w_r.shape[1], w_m.shape[1], w_g.shape[1]
    const = lambda i: (0, 0)
    return pl.pallas_call(
        _mixin_kernel,
        grid=(n // tm,),
        in_specs=[
            pl.BlockSpec((tm, dm), lambda i: (i, 0)),
            pl.BlockSpec((1, 9, dm), lambda i: ((i * tm) // seq, 0, 0)),
            pl.BlockSpec((1, dm), const),
            pl.BlockSpec((dm, nr), const),
            pl.BlockSpec((dm, nm), const),
            pl.BlockSpec((dm, ng), const),
        ],
        out_specs=[
            pl.BlockSpec((tm, nr), lambda i: (i, 0)),
            pl.BlockSpec((tm, nm), lambda i: (i, 0)),
            pl.BlockSpec((tm, ng), lambda i: (i, 0)),
        ],
        out_shape=[jax.ShapeDtypeStruct((n, nr), F32), jax.ShapeDtypeStruct((n, nm), F32),
                   jax.ShapeDtypeStruct((n, ng), F32)],
        compiler_params=pltpu.CompilerParams(
            dimension_semantics=("parallel",), vmem_limit_bytes=VMEM_LIMIT),
        name="mix_in",
    )(x2, mod_l, norm_g, w_r, w_m, w_g)


def _rwkv_kernel(p_ref, mu_ref, w0_ref, wup_ref, a0_ref, aup_ref, gup_ref, kk_ref, ka_ref, rk_ref,
                 lng_ref, lnb_ref, o_ref,
                 prev_ref, s_ref, r_s, ld_s, k_s, v_s, a_s, b_s, y_s, bonus_s, g_s):
    tc = p_ref.shape[0]
    width = o_ref.shape[1]
    C = RWKV_CHUNK
    HD = RWKV_HEAD_DIM
    HG = RWKV_GROUP_HEADS
    GL = HG * HD
    n_groups = width // GL
    shift = HD.bit_length() - 1
    i = pl.program_id(1)

    @pl.when(i == 0)
    def _():
        prev_ref[...] = jnp.zeros_like(prev_ref)
        s_ref[...] = jnp.zeros_like(s_ref)

    pf = p_ref[...]
    rolled = pltpu.roll(pf, 1, axis=0)
    row = lax.broadcasted_iota(jnp.int32, (tc, 1), 0)
    prev = jnp.where(row == 0, prev_ref[...], rolled)
    prev_ref[...] = pf[tc - 1:tc, :]
    xs = pf + (prev - pf) * mu_ref[...]

    r = xs[:, 0:width]
    k = xs[:, width:2 * width]
    v = xs[:, 2 * width:3 * width]
    low = xs[:, 3 * width:3 * width + RWKV_LOWRANK_PAD]

    rg = lax.broadcasted_iota(jnp.int32, (GL, GL), 0)
    cg = lax.broadcasted_iota(jnp.int32, (GL, GL), 1)
    blockmask = (rg >> shift) == (cg >> shift)
    eye = rg == cg
    ones_bd = jnp.where(blockmask, 1.0, 0.0).astype(BF16)

    def segsum(x):
        return jnp.concatenate(
            [_dot_exact_rhs(x[:, g * GL:(g + 1) * GL], ones_bd, 2) for g in range(n_groups)], axis=1)

    w = w0_ref[...] + _bdot(jnp.tanh(low), wup_ref[...])
    w = -_softplus(-w) - 0.5
    ld_s[...] = -jnp.exp(w)
    asig = jax.nn.sigmoid(a0_ref[...] + _bdot(low, aup_ref[...]))
    g_s[...] = _bdot(jax.nn.sigmoid(low), gup_ref[...])
    kk = k * kk_ref[...]
    kk = kk / jnp.maximum(jnp.sqrt(segsum(kk * kk)), L2_EPS)
    k2 = k * (1.0 + (asig - 1.0) * ka_ref[...])
    bonus_s[...] = segsum(r * k2 * rk_ref[...]) * v
    r_s[...] = r
    k_s[...] = k2
    v_s[...] = v
    a_s[...] = -kk
    b_s[...] = kk * asig

    lane = lax.broadcasted_iota(jnp.int32, (C, GL), 1)
    tok = lax.broadcasted_iota(jnp.int32, (C, GL), 0)
    strict = (lane & (HD - 1)) < tok
    incl = (lane & (HD - 1)) <= tok
    headmask = [(lane >> shift) == h for h in range(HG)]
    tri = (lax.broadcasted_iota(jnp.int32, (C, C), 1) <= lax.broadcasted_iota(jnp.int32, (C, C), 0))
    tri = jnp.where(tri, 1.0, 0.0).astype(BF16)

    def bd(x):
        return jnp.where(blockmask, jnp.concatenate([x] * HG, axis=0), 0.0).astype(BF16)

    def chunk(c, carry):
        sl = pl.ds(pl.multiple_of(c * C, C), C)
        ld = ld_s[sl, :]
        lp = _exact_lhs_dot(tri, ld, 3)
        lpc = lp[C - 1:C, :]
        e_lp = jnp.exp(lp)
        e_nlp = jnp.exp(-lp)
        e_end = jnp.exp(lpc - lp)
        rt = r_s[sl, :] * e_lp
        at = a_s[sl, :] * jnp.exp(lp - ld)
        bb = b_s[sl, :]
        kc = k_s[sl, :]
        bt = bb * e_nlp
        kt = kc * e_nlp
        bh = bb * e_end
        kh = kc * e_end
        vv = v_s[sl, :]
        pc = jnp.exp(lpc)
        for g in range(n_groups):
            gs = slice(g * GL, (g + 1) * GL)
            at_g, rt_g, v_g = at[:, gs], rt[:, gs], vv[:, gs]
            lhs = jnp.concatenate([at_g, rt_g], axis=0)
            rows = [jnp.where(headmask[h], bt[:, gs], 0.0) for h in range(HG)]
            rows += [jnp.where(headmask[h], kt[:, gs], 0.0) for h in range(HG)]
            gram = _bdot_nt(lhs, jnp.concatenate(rows, axis=0))
            a_ab = jnp.where(strict, gram[:C, :GL], 0.0)
            a_ak = jnp.where(strict, gram[:C, GL:], 0.0)
            a_rb = jnp.where(incl, gram[C:, :GL], 0.0)
            a_rk = jnp.where(incl, gram[C:, GL:], 0.0)
            bdv = bd(v_g)
            x1 = at_g
            x2 = _bdot(a_ak, bdv)
            nm = a_ab.astype(BF16)
            levels = C.bit_length() - 1
            for lvl in range(levels):
                upd = jnp.dot(nm, jnp.concatenate([bd(x1), bd(x2)], axis=1),
                              preferred_element_type=F32)
                x1 = x1 + upd[:, :GL]
                x2 = x2 + upd[:, GL:]
                if lvl + 1 < levels:
                    nm = jnp.dot(nm, bd(nm.astype(F32)), preferred_element_type=F32).astype(BF16)
            s_old = s_ref[g]
            s_bf = s_old.astype(BF16)
            u = jnp.dot(x1.astype(BF16), s_bf, preferred_element_type=F32) + x2
            y = jnp.dot(rt_g.astype(BF16), s_bf, preferred_element_type=F32)
            y = y + _bdot(jnp.concatenate([a_rb, a_rk], axis=1),
                          jnp.concatenate([bd(u), bdv], axis=0))
            upd = _bdot_tn(jnp.concatenate([bh[:, gs], kh[:, gs]], axis=0),
                           jnp.concatenate([u, v_g], axis=0))
            pc_col = jnp.sum(jnp.where(eye, jnp.broadcast_to(pc[:, gs], (GL, GL)), 0.0),
                             axis=1, keepdims=True)
            s_ref[g] = pc_col * s_old + jnp.where(blockmask, upd, 0.0)
            y_s[sl, gs] = y
        return carry

    lax.fori_loop(0, tc // C, chunk, 0)

    y = y_s[...]
    inv = 1.0 / HD
    yc = y - segsum(y) * inv
    var = segsum(yc * yc) * inv
    yn = yc * lax.rsqrt(var + RWKV_GN_EPS) * lng_ref[...] + lnb_ref[...]
    o_ref[...] = (yn + bonus_s[...]) * g_s[...]


def _rwkv_call(p_r, mu, w0, wup, a0, aup, gup, k_k, k_a, r_k, ln_g, ln_b, *, batch, seq, tc):
    n, ncols = p_r.shape
    width = RWKV_HEADS * RWKV_HEAD_DIM
    gl = RWKV_GROUP_HEADS * RWKV_HEAD_DIM
    nt = seq // tc
    const = lambda b, i: (0, 0)
    vec = pl.BlockSpec((1, width), const)
    lowmat = pl.BlockSpec((RWKV_LOWRANK_PAD, width), const)
    tile = pltpu.VMEM((tc, width), F32)
    return pl.pallas_call(
        _rwkv_kernel,
        grid=(batch, nt),
        in_specs=[
            pl.BlockSpec((tc, ncols), lambda b, i: (b * nt + i, 0)),
            pl.BlockSpec((1, ncols), const),
            vec, lowmat, vec, lowmat, lowmat, vec, vec, vec, vec, vec,
        ],
        out_specs=pl.BlockSpec((tc, width), lambda b, i: (b * nt + i, 0)),
        out_shape=jax.ShapeDtypeStruct((n, width), F32),
        scratch_shapes=[
            pltpu.VMEM((1, ncols), F32),
            pltpu.VMEM((width // gl, gl, gl), F32),
            tile, tile, tile, tile, tile, tile, tile, tile, tile,
        ],
        compiler_params=pltpu.CompilerParams(
            dimension_semantics=("parallel", "arbitrary"), vmem_limit_bytes=VMEM_LIMIT),
        name="rwkv7",
    )(p_r, mu, w0, wup, a0, aup, gup, k_k, k_a, r_k, ln_g, ln_b)


def _mlstm_kernel(p_ref, cw_ref, cb_ref, gb_ref, lng_ref, o_ref, xbuf, c_ref, n_ref, m_ref, qk_s):
    tcm = p_ref.shape[0]
    NH, DH, L, KC = MLSTM_HEADS, MLSTM_HEAD_DIM, MLSTM_CHUNK, MLSTM_CONV
    wd = NH * DH
    i = pl.program_id(1)

    @pl.when(i == 0)
    def _():
        xbuf[0:SUBLANES, :] = jnp.zeros((SUBLANES, 2 * wd), F32)
        c_ref[...] = jnp.zeros_like(c_ref)
        n_ref[...] = jnp.zeros_like(n_ref)
        m_ref[...] = jnp.full(m_ref.shape, -jnp.inf, F32)

    @pl.when(i > 0)
    def _():
        xbuf[0:SUBLANES, :] = xbuf[tcm:tcm + SUBLANES, :]

    xbuf[SUBLANES:SUBLANES + tcm, :] = p_ref[:, 0:2 * wd]
    acc = cb_ref[...] + cw_ref[KC - 1:KC, :] * xbuf[SUBLANES:SUBLANES + tcm, :]
    for j in range(KC - 1):
        off = SUBLANES - (KC - 1) + j
        acc = acc + cw_ref[j:j + 1, :] * xbuf[off:off + tcm, :]
    qk_s[...] = acc * jax.nn.sigmoid(acc)

    causal = (lax.broadcasted_iota(jnp.int32, (L, L), 1) <= lax.broadcasted_iota(jnp.int32, (L, L), 0))
    tri = jnp.where(causal, 1.0, 0.0).astype(BF16)
    scale = DH ** -0.5

    def chunk(c, carry):
        sl = pl.ds(pl.multiple_of(c * L, L), L)
        gates = p_ref[sl, 4 * wd:4 * wd + LANES] + gb_ref[...]
        log_f = -_softplus(-gates)
        b_col = _exact_lhs_dot(tri, log_f, 3)
        b_row = b_col.T
        li_row = gates.T
        for h in range(NH):
            hs = slice(h * DH, (h + 1) * DH)
            q = qk_s[sl, hs] * scale
            kh = qk_s[sl, wd + h * DH:wd + (h + 1) * DH]
            vh = p_ref[sl, 2 * wd + h * DH:2 * wd + (h + 1) * DH]
            bc = b_col[:, NH + h:NH + h + 1]
            br = b_row[NH + h:NH + h + 1, :]
            lic = gates[:, h:h + 1]
            lir = li_row[h:h + 1, :]
            b_end = bc[L - 1:L, :]
            m_prev = m_ref[h][:, 0:1]
            c_prev = c_ref[h]
            n_prev = n_ref[h]

            d = jnp.where(causal, bc - br + lir, -jnp.inf)
            m_inter = bc + m_prev
            m_t = jnp.maximum(m_inter, jnp.max(d, axis=1, keepdims=True))
            scores = _bdot_nt(q, kh) * jnp.exp(d - m_t)
            inter = jnp.exp(m_inter - m_t)
            num = _bdot(scores, vh) + inter * _bdot(q, c_prev)
            den = jnp.sum(scores, axis=1, keepdims=True) + inter * jnp.sum(q * n_prev, axis=1, keepdims=True)
            hh = num / jnp.maximum(jnp.abs(den), jnp.exp(-m_t))

            g_end = b_end - bc + lic
            m_new = jnp.maximum(b_end + m_prev, jnp.max(g_end, axis=0, keepdims=True))
            keep = jnp.exp(b_end + m_prev - m_new)
            kw = kh * jnp.exp(g_end - m_new)
            c_ref[h] = keep * c_prev + _bdot_tn(kw, vh)
            n_ref[h] = keep * n_prev + jnp.sum(kw, axis=0, keepdims=True)
            m_ref[h] = jnp.broadcast_to(m_new, (1, LANES))

            hc = hh - jnp.mean(hh, axis=1, keepdims=True)
            hn = hc * lax.rsqrt(jnp.mean(hc * hc, axis=1, keepdims=True) + MLSTM_NORM_EPS)
            og = jax.nn.sigmoid(p_ref[sl, 3 * wd + h * DH:3 * wd + (h + 1) * DH])
            o_ref[sl, hs] = og * (hn * lng_ref[:, hs])
        return carry

    lax.fori_loop(0, tcm // L, chunk, 0)


def _mlstm_call(p_m, conv_w, conv_b, gate_b, ln_g, *, batch, seq, tcm):
    n, ncols = p_m.shape
    nh, dh = MLSTM_HEADS, MLSTM_HEAD_DIM
    wd = nh * dh
    nt = seq // tcm
    const = lambda b, i: (0, 0)
    return pl.pallas_call(
        _mlstm_kernel,
        grid=(batch, nt),
        in_specs=[
            pl.BlockSpec((tcm, ncols), lambda b, i: (b * nt + i, 0)),
            pl.BlockSpec((MLSTM_CONV, 2 * wd), const),
            pl.BlockSpec((1, 2 * wd), const),
            pl.BlockSpec((1, LANES), const),
            pl.BlockSpec((1, wd), const),
        ],
        out_specs=pl.BlockSpec((tcm, wd), lambda b, i: (b * nt + i, 0)),
        out_shape=jax.ShapeDtypeStruct((n, wd), F32),
        scratch_shapes=[
            pltpu.VMEM((tcm + SUBLANES, 2 * wd), F32),
            pltpu.VMEM((nh, dh, dh), F32),
            pltpu.VMEM((nh, 1, dh), F32),
            pltpu.VMEM((nh, 1, LANES), F32),
            pltpu.VMEM((tcm, 2 * wd), F32),
        ],
        compiler_params=pltpu.CompilerParams(
            dimension_semantics=("parallel", "arbitrary"), vmem_limit_bytes=VMEM_LIMIT),
        name="mlstm",
    )(p_m, conv_w, conv_b, gate_b, ln_g)


def _mixout_kernel(x_ref, yr_ref, ym_ref, pg_ref, mod_ref, wr_ref, wm_ref, wo_ref, o_ref):
    dm = x_ref.shape[1]
    zr = jnp.dot(yr_ref[...].astype(BF16), wr_ref[...], preferred_element_type=F32)
    zm = jnp.dot(ym_ref[...].astype(BF16), wm_ref[...], preferred_element_type=F32)
    mix = jax.nn.sigmoid(pg_ref[:, 0:dm]) * zr + jax.nn.sigmoid(pg_ref[:, dm:2 * dm]) * zm
    out = jnp.dot(mix.astype(BF16), wo_ref[...], preferred_element_type=F32)
    o_ref[...] = x_ref[...] + (1.0 + mod_ref[0, 5:6, :]) * out


def _mixout_call(x2, y_r, y_m, p_g, mod_l, w_r, w_m, w_o, *, seq, tm):
    n, dm = x2.shape
    const = lambda i: (0, 0)
    rowblk = lambda cols: pl.BlockSpec((tm, cols), lambda i: (i, 0))
    return pl.pallas_call(
        _mixout_kernel,
        grid=(n // tm,),
        in_specs=[
            rowblk(dm), rowblk(y_r.shape[1]), rowblk(y_m.shape[1]), rowblk(p_g.shape[1]),
            pl.BlockSpec((1, 9, dm), lambda i: ((i * tm) // seq, 0, 0)),
            pl.BlockSpec(w_r.shape, const), pl.BlockSpec(w_m.shape, const), pl.BlockSpec(w_o.shape, const),
        ],
        out_specs=rowblk(dm),
        out_shape=jax.ShapeDtypeStruct((n, dm), F32),
        compiler_params=pltpu.CompilerParams(
            dimension_semantics=("parallel",), vmem_limit_bytes=VMEM_LIMIT),
        name="mix_out",
    )(x2, y_r, y_m, p_g, mod_l, w_r, w_m, w_o)


def _pad_cols(a, n):
    return jnp.pad(a, ((0, 0), (0, n - a.shape[1])))


def _row(a, n=None):
    a = a.reshape(1, -1)
    return a if n is None else _pad_cols(a, n)


def _lowrank_pad(mat, row0):
    out = jnp.zeros((RWKV_LOWRANK_PAD, mat.shape[1]), F32)
    return lax.dynamic_update_slice(out, mat, (row0, 0)).astype(BF16)


def kernel(x, c, ada_w, ada_b, norm_g, ffn_w_in, ffn_w_out, mix_w_in, rwkv_mu, rwkv_w0, rwkv_w_up, rwkv_a0, rwkv_a_up, rwkv_g_up, rwkv_k_k, rwkv_k_a, rwkv_r_k, rwkv_ln_g, rwkv_ln_b, mlstm_conv_w, mlstm_conv_b, mlstm_gate_b, mlstm_ln_g, branch_w_rwkv, branch_w_mlstm, mix_w_out, final_g):
    batch, seq, dm = x.shape
    depth = ada_w.shape[0]
    n = batch * seq
    rw = RWKV_HEADS * RWKV_HEAD_DIM
    mw = MLSTM_HEADS * MLSTM_HEAD_DIM
    rwkv_cols = 3 * rw + RWKV_DECAY_RANK + RWKV_A_RANK + RWKV_GATE_RANK
    mlstm_cols = 4 * mw + 2 * MLSTM_HEADS
    rwkv_pad = 3 * rw + RWKV_LOWRANK_PAD
    mlstm_pad = 4 * mw + LANES

    tm_ffn = min(1024, seq)
    tm_mix = min(256, seq)
    tc_rwkv = min(512, seq)
    tc_mlstm = min(512, seq)

    c_pad = jnp.pad(c, ((0, SUBLANES - batch % SUBLANES if batch % SUBLANES else 0), (0, 0)))
    mod = _ada_call(c_pad, ada_w, ada_b)[:, :batch].reshape(depth, batch, 9, dm)

    ffn_w_in_bf = ffn_w_in.astype(BF16)
    ffn_w_out_bf = ffn_w_out.astype(BF16)
    final_row = _row(final_g)

    x2 = x.reshape(n, dm)
    for l in range(depth):
        mod_l = mod[l]
        x2 = _ffn_call(x2, mod_l, _row(norm_g[l, 0]), ffn_w_in_bf[l, 0], ffn_w_out_bf[l, 0], final_row,
                       seq=seq, mod_row=0, final_norm=False, tm=tm_ffn, tf=256)

        w_in = mix_w_in[l]
        w_r = _pad_cols(w_in[:, :rwkv_cols], rwkv_pad).astype(BF16)
        w_m = _pad_cols(w_in[:, rwkv_cols:rwkv_cols + mlstm_cols], mlstm_pad).astype(BF16)
        w_g = w_in[:, rwkv_cols + mlstm_cols:].astype(BF16)
        p_r, p_m, p_g = _mixin_call(x2, mod_l, _row(norm_g[l, 1]), w_r, w_m, w_g, seq=seq, tm=tm_mix)

        y_r = _rwkv_call(
            p_r, _row(rwkv_mu[l], rwkv_pad), _row(rwkv_w0[l]),
            _lowrank_pad(rwkv_w_up[l], 0), _row(rwkv_a0[l]),
            _lowrank_pad(rwkv_a_up[l], RWKV_DECAY_RANK),
            _lowrank_pad(rwkv_g_up[l], RWKV_DECAY_RANK + RWKV_A_RANK),
            _row(rwkv_k_k[l]), _row(rwkv_k_a[l]), _row(rwkv_r_k[l]), _row(rwkv_ln_g[l]), _row(rwkv_ln_b[l]),
            batch=batch, seq=seq, tc=tc_rwkv)
        y_m = _mlstm_call(p_m, mlstm_conv_w[l], _row(mlstm_conv_b[l]), _row(mlstm_gate_b[l], LANES),
                          _row(mlstm_ln_g[l]), batch=batch, seq=seq, tcm=tc_mlstm)

        x2 = _mixout_call(x2, y_r, y_m, p_g, mod_l, branch_w_rwkv[l].astype(BF16),
                          branch_w_mlstm[l].astype(BF16), mix_w_out[l].astype(BF16), seq=seq, tm=512)

        x2 = _ffn_call(x2, mod_l, _row(norm_g[l, 2]), ffn_w_in_bf[l, 1], ffn_w_out_bf[l, 1], final_row,
                       seq=seq, mod_row=6, final_norm=(l == depth - 1), tm=tm_ffn, tf=256)
    return x2.reshape(batch, seq, dm)
```

```python
import functools

import jax
import jax.numpy as jnp
from jax import lax
from jax.experimental import pallas as pl
from jax.experimental.pallas import tpu as pltpu

F32 = jnp.float32
BF16 = jnp.bfloat16

RWKV_HEADS = 8
RWKV_HEAD_DIM = 64
RWKV_DECAY_RANK = 32
RWKV_A_RANK = 32
RWKV_GATE_RANK = 96
MLSTM_HEADS = 4
MLSTM_HEAD_DIM = 128
MLSTM_CONV = 4
MLSTM_CHUNK = 128
RMS_EPS = 1e-6
RWKV_GN_EPS = 64e-5
MLSTM_NORM_EPS = 1e-5
L2_EPS = 1e-12

LANES = 128
SUBLANES = 8
VMEM_LIMIT = 56 * 1024 * 1024

RWKV_CHUNK = 64
RWKV_GROUP_HEADS = 4
RWKV_PREP_UNROLL = 4
RWKV_LOWRANK_PAD = 256


def _bdot(a, b):
    return jnp.dot(a.astype(BF16), b.astype(BF16), preferred_element_type=F32)


def _bdot_nt(a, b):
    return lax.dot_general(a.astype(BF16), b.astype(BF16), (((1,), (1,)), ((), ())),
                           preferred_element_type=F32)


def _bdot_tn(a, b):
    return lax.dot_general(a.astype(BF16), b.astype(BF16), (((0,), (0,)), ((), ())),
                           preferred_element_type=F32)


def _split_bf16(x, parts):
    out = []
    rem = x
    for _ in range(parts):
        p = rem.astype(BF16)
        out.append(p)
        rem = rem - p.astype(F32)
    return out


def _dot_exact_rhs(x, m_bf16, parts):
    acc = None
    for p in _split_bf16(x, parts):
        t = jnp.dot(p, m_bf16, preferred_element_type=F32)
        acc = t if acc is None else acc + t
    return acc


def _exact_lhs_dot(m_bf16, x, parts):
    acc = None
    for p in _split_bf16(x, parts):
        t = jnp.dot(m_bf16, p, preferred_element_type=F32)
        acc = t if acc is None else acc + t
    return acc


def _rms_mod(x, g, shift, scale):
    y = x * lax.rsqrt(jnp.mean(x * x, axis=-1, keepdims=True) + RMS_EPS) * g
    return y * (1.0 + scale) + shift


def _softplus(z):
    return jnp.maximum(z, 0.0) + jnp.log1p(jnp.exp(-jnp.abs(z)))


def _ada_kernel(c_ref, w_ref, b_ref, o_ref):
    c = c_ref[...]
    cond = c * jax.nn.sigmoid(c)
    o_ref[0] = _bdot(cond, w_ref[0]) + b_ref[0]


def _ada_call(c_pad, ada_w, ada_b):
    depth, dm, n9 = ada_w.shape
    tn = n9 // 4
    rows = c_pad.shape[0]
    return pl.pallas_call(
        _ada_kernel,
        grid=(depth, n9 // tn),
        in_specs=[
            pl.BlockSpec((rows, dm), lambda l, j: (0, 0)),
            pl.BlockSpec((1, dm, tn), lambda l, j: (l, 0, j)),
            pl.BlockSpec((1, 1, tn), lambda l, j: (l, 0, j)),
        ],
        out_specs=pl.BlockSpec((1, rows, tn), lambda l, j: (l, 0, j)),
        out_shape=jax.ShapeDtypeStruct((depth, rows, n9), F32),
        compiler_params=pltpu.CompilerParams(
            dimension_semantics=("arbitrary", "arbitrary"), vmem_limit_bytes=VMEM_LIMIT),
        name="ada_mod",
    )(c_pad, ada_w, ada_b.reshape(depth, 1, n9))


def _ffn_kernel(x_ref, mod_ref, g_ref, wg_ref, wu_ref, wo_ref, fg_ref, o_ref, xn_ref, acc_ref,
                *, mod_row, final_norm):
    j = pl.program_id(1)

    @pl.when(j == 0)
    def _():
        h = _rms_mod(x_ref[...], g_ref[...], mod_ref[0, mod_row:mod_row + 1, :],
                     mod_ref[0, mod_row + 1:mod_row + 2, :])
        xn_ref[...] = h.astype(BF16)
        acc_ref[...] = jnp.zeros_like(acc_ref)

    xn = xn_ref[...]
    gate = jnp.dot(xn, wg_ref[...], preferred_element_type=F32)
    up = jnp.dot(xn, wu_ref[...], preferred_element_type=F32)
    act = (gate * jax.nn.sigmoid(gate)) * up
    acc_ref[...] += jnp.dot(act.astype(BF16), wo_ref[...], preferred_element_type=F32)

    @pl.when(j == pl.num_programs(1) - 1)
    def _():
        gt = mod_ref[0, mod_row + 2:mod_row + 3, :]
        out = x_ref[...] + (0.5 * (1.0 + gt)) * acc_ref[...]
        if final_norm:
            out = out * lax.rsqrt(jnp.mean(out * out, axis=-1, keepdims=True) + RMS_EPS) * fg_ref[...]
        o_ref[...] = out


def _ffn_call(x2, mod_l, norm_g, w_in, w_out, final_g, *, seq, mod_row, final_norm, tm, tf):
    n, dm = x2.shape
    dff = w_out.shape[0]
    nf = dff // tf
    kern = functools.partial(_ffn_kernel, mod_row=mod_row, final_norm=final_norm)
    return pl.pallas_call(
        kern,
        grid=(n // tm, nf),
        in_specs=[
            pl.BlockSpec((tm, dm), lambda i, j: (i, 0)),
            pl.BlockSpec((1, 9, dm), lambda i, j: ((i * tm) // seq, 0, 0)),
            pl.BlockSpec((1, dm), lambda i, j: (0, 0)),
            pl.BlockSpec((dm, tf), lambda i, j: (0, j)),
            pl.BlockSpec((dm, tf), lambda i, j: (0, j + nf)),
            pl.BlockSpec((tf, dm), lambda i, j: (j, 0)),
            pl.BlockSpec((1, dm), lambda i, j: (0, 0)),
        ],
        out_specs=pl.BlockSpec((tm, dm), lambda i, j: (i, 0)),
        out_shape=jax.ShapeDtypeStruct((n, dm), F32),
        scratch_shapes=[pltpu.VMEM((tm, dm), BF16), pltpu.VMEM((tm, dm), F32)],
        compiler_params=pltpu.CompilerParams(
            dimension_semantics=("parallel", "arbitrary"), vmem_limit_bytes=VMEM_LIMIT),
        name="ffn",
    )(x2, mod_l, norm_g, w_in, w_in, w_out, final_g)


def _mixin_kernel(x_ref, mod_ref, g_ref, wr_ref, wm_ref, wg_ref, pr_ref, pm_ref, pg_ref):
    h = _rms_mod(x_ref[...], g_ref[...], mod_ref[0, 3:4, :], mod_ref[0, 4:5, :]).astype(BF16)
    pr_ref[...] = jnp.dot(h, wr_ref[...], preferred_element_type=F32)
    pm_ref[...] = jnp.dot(h, wm_ref[...], preferred_element_type=F32)
    pg_ref[...] = jnp.dot(h, wg_ref[...], preferred_element_type=F32)


def _mixin_call(x2, mod_l, norm_g, w_r, w_m, w_g, *, seq, tm):
    n, dm = x2.shape
    nr, nm, ng = w_r.shape[1], w_m.shape[1], w_g.shape[1]
    const = lambda i: (0, 0)
    return pl.pallas_call(
        _mixin_kernel,
        grid=(n // tm,),
        in_specs=[
            pl.BlockSpec((tm, dm), lambda i: (i, 0)),
            pl.BlockSpec((1, 9, dm), lambda i: ((i * tm) // seq, 0, 0)),
            pl.BlockSpec((1, dm), const),
            pl.BlockSpec((dm, nr), const),
            pl.BlockSpec((dm, nm), const),
            pl.BlockSpec((dm, ng), const),
        ],
        out_specs=[
            pl.BlockSpec((tm, nr), lambda i: (i, 0)),
            pl.BlockSpec((tm, nm), lambda i: (i, 0)),
            pl.BlockSpec((tm, ng), lambda i: (i, 0)),
        ],
        out_shape=[jax.ShapeDtypeStruct((n, nr), F32), jax.ShapeDtypeStruct((n, nm), F32),
                   jax.ShapeDtypeStruct((n, ng), F32)],
        compiler_params=pltpu.CompilerParams(
            dimension_semantics=("parallel",), vmem_limit_bytes=VMEM_LIMIT),
        name="mix_in",
    )(x2, mod_l, norm_g, w_r, w_m, w_g)


def _rwkv_kernel(p_ref, mu_ref, w0_ref, wup_ref, a0_ref, aup_ref, gup_ref, kk_ref, ka_ref, rk_ref,
                 lng_ref, lnb_ref, o_ref,
                 prev_ref, s_ref, r_s, ld_s, k_s, v_s, a_s, b_s, y_s, bonus_s, g_s, mx_s, n0_s, pc_s):
    tc = p_ref.shape[0]
    width = o_ref.shape[1]
    C = RWKV_CHUNK
    HD = RWKV_HEAD_DIM
    HG = RWKV_GROUP_HEADS
    GL = HG * HD
    n_groups = width // GL
    shift = HD.bit_length() - 1
    i = pl.program_id(1)

    @pl.when(i == 0)
    def _():
        prev_ref[...] = jnp.zeros_like(prev_ref)
        s_ref[...] = jnp.zeros_like(s_ref)

    pf = p_ref[...]
    rolled = pltpu.roll(pf, 1, axis=0)
    row = lax.broadcasted_iota(jnp.int32, (tc, 1), 0)
    prev = jnp.where(row == 0, prev_ref[...], rolled)
    prev_ref[...] = pf[tc - 1:tc, :]
    xs = pf + (prev - pf) * mu_ref[...]

    r = xs[:, 0:width]
    k = xs[:, width:2 * width]
    v = xs[:, 2 * width:3 * width]
    low = xs[:, 3 * width:3 * width + RWKV_LOWRANK_PAD]

    rg = lax.broadcasted_iota(jnp.int32, (GL, GL), 0)
    cg = lax.broadcasted_iota(jnp.int32, (GL, GL), 1)
    blockmask = (rg >> shift) == (cg >> shift)
    eye = rg == cg
    ones_bd = jnp.where(blockmask, 1.0, 0.0).astype(BF16)

    def segsum(x):
        return jnp.concatenate(
            [_dot_exact_rhs(x[:, g * GL:(g + 1) * GL], ones_bd, 2) for g in range(n_groups)], axis=1)

    w = w0_ref[...] + _bdot(jnp.tanh(low), wup_ref[...])
    w = -_softplus(-w) - 0.5
    ld_s[...] = -jnp.exp(w)
    asig = jax.nn.sigmoid(a0_ref[...] + _bdot(low, aup_ref[...]))
    g_s[...] = _bdot(jax.nn.sigmoid(low), gup_ref[...])
    kk = k * kk_ref[...]
    kk = kk / jnp.maximum(jnp.sqrt(segsum(kk * kk)), L2_EPS)
    k2 = k * (1.0 + (asig - 1.0) * ka_ref[...])
    bonus_s[...] = segsum(r * k2 * rk_ref[...]) * v
    r_s[...] = r
    k_s[...] = k2
    v_s[...] = v
    a_s[...] = -kk
    b_s[...] = kk * asig

    lane = lax.broadcasted_iota(jnp.int32, (C, GL), 1)
    tok = lax.broadcasted_iota(jnp.int32, (C, GL), 0)
    strict = (lane & (HD - 1)) < tok
    incl = (lane & (HD - 1)) <= tok
    headmask = [(lane >> shift) == h for h in range(HG)]
    tri = (lax.broadcasted_iota(jnp.int32, (C, C), 1) <= lax.broadcasted_iota(jnp.int32, (C, C), 0))
    tri = jnp.where(tri, 1.0, 0.0).astype(BF16)

    def bd(x):
        return jnp.where(blockmask, jnp.concatenate([x] * HG, axis=0), 0.0).astype(BF16)

    levels = C.bit_length() - 1

    def prepare_body(c2, carry):
        inst = []
        for u in range(RWKV_PREP_UNROLL):
            c = c2 * RWKV_PREP_UNROLL + u
            sl = pl.ds(pl.multiple_of(c * C, C), C)
            ld = ld_s[sl, :]
            lp = _exact_lhs_dot(tri, ld, 3)
            lpc = lp[C - 1:C, :]
            e_lp = jnp.exp(lp)
            e_nlp = jnp.exp(-lp)
            e_end = jnp.exp(lpc - lp)
            rt = r_s[sl, :] * e_lp
            at = a_s[sl, :] * jnp.exp(lp - ld)
            bb = b_s[sl, :]
            kc = k_s[sl, :]
            bt = bb * e_nlp
            kt = kc * e_nlp
            bh = bb * e_end
            kh = kc * e_end
            vv = v_s[sl, :]
            pc = jnp.exp(lpc)
            for g in range(n_groups):
                gs = slice(g * GL, (g + 1) * GL)
                inst.append(dict(sl=sl, gs=gs, idx=c * n_groups + g, at=at[:, gs], rt=rt[:, gs], v=vv[:, gs],
                                 bt=bt[:, gs], kt=kt[:, gs], bh=bh[:, gs], kh=kh[:, gs], pc=pc[:, gs]))
        for t in inst:
            lhs = jnp.concatenate([t['at'], t['rt']], axis=0)
            rows = [jnp.where(headmask[h], t['bt'], 0.0) for h in range(HG)]
            rows += [jnp.where(headmask[h], t['kt'], 0.0) for h in range(HG)]
            gram = _bdot_nt(lhs, jnp.concatenate(rows, axis=0))
            t['a_ab'] = jnp.where(strict, gram[:C, :GL], 0.0)
            t['a_ak'] = jnp.where(strict, gram[:C, GL:], 0.0)
            t['a_rb'] = jnp.where(incl, gram[C:, :GL], 0.0).astype(BF16)
            t['a_rk'] = jnp.where(incl, gram[C:, GL:], 0.0).astype(BF16)
        for t in inst:
            t['bdv'] = bd(t['v'])
            t['akv'] = _bdot(t['a_ak'], t['bdv'])
            t['nm'] = t['a_ab']
            t['tr'] = t['a_ab']
            t['nbd'] = bd(t['nm'])
        for _ in range(1, levels):
            for t in inst:
                t['nm'] = jnp.dot(t['nm'].astype(BF16), t['nbd'], preferred_element_type=F32)
                t['nbd'] = bd(t['nm'])
            for t in inst:
                t['tr'] = t['tr'] + t['nm'] + _bdot(t['tr'], t['nbd'])
        for t in inst:
            wu = _bdot(t['tr'], jnp.concatenate([bd(t['at']), bd(t['akv'])], axis=1))
            t['w'] = t['at'] + wu[:, :GL]
            t['u0'] = t['akv'] + wu[:, GL:]
        for t in inst:
            ry = jnp.dot(t['a_rb'], jnp.concatenate([bd(t['w']), bd(t['u0'])], axis=1),
                         preferred_element_type=F32)
            r_s[t['sl'], t['gs']] = t['rt'] + ry[:, :GL]
            y_s[t['sl'], t['gs']] = ry[:, GL:] + jnp.dot(t['a_rk'], t['bdv'], preferred_element_type=F32)
        for t in inst:
            mn = _bdot_tn(t['bh'], jnp.concatenate([t['w'], t['u0']], axis=1))
            mx_s[t['idx']] = jnp.where(blockmask, mn[:, :GL], 0.0).astype(BF16)
            n0_s[t['idx']] = jnp.where(blockmask, mn[:, GL:] + _bdot_tn(t['kh'], t['v']), 0.0)
            pc_s[t['idx']] = jnp.broadcast_to(
                jnp.sum(jnp.where(eye, jnp.broadcast_to(t['pc'], (GL, GL)), 0.0), axis=1, keepdims=True),
                (GL, LANES))
        return carry

    lax.fori_loop(0, tc // (C * RWKV_PREP_UNROLL), prepare_body, 0)

    def advance(c, carry):
        sl = pl.ds(pl.multiple_of(c * C, C), C)
        for g in range(n_groups):
            gs = slice(g * GL, (g + 1) * GL)
            idx = c * n_groups + g
            s_old = s_ref[g]
            s_bf = s_old.astype(BF16)
            y_s[sl, gs] = y_s[sl, gs] + jnp.dot(r_s[sl, gs].astype(BF16), s_bf, preferred_element_type=F32)
            s_ref[g] = (pc_s[idx][:, 0:1] * s_old + n0_s[idx]
                        + jnp.dot(mx_s[idx], s_bf, preferred_element_type=F32))
        return carry

    lax.fori_loop(0, tc // C, advance, 0)

    y = y_s[...]
    inv = 1.0 / HD
    yc = y - segsum(y) * inv
    var = segsum(yc * yc) * inv
    yn = yc * lax.rsqrt(var + RWKV_GN_EPS) * lng_ref[...] + lnb_ref[...]
    o_ref[...] = (yn + bonus_s[...]) * g_s[...]


def _rwkv_call(p_r, mu, w0, wup, a0, aup, gup, k_k, k_a, r_k, ln_g, ln_b, *, batch, seq, tc):
    n, ncols = p_r.shape
    width = RWKV_HEADS * RWKV_HEAD_DIM
    gl = RWKV_GROUP_HEADS * RWKV_HEAD_DIM
    nt = seq // tc
    nblk = (tc // RWKV_CHUNK) * (width // gl)
    const = lambda b, i: (0, 0)
    vec = pl.BlockSpec((1, width), const)
    lowmat = pl.BlockSpec((RWKV_LOWRANK_PAD, width), const)
    tile = pltpu.VMEM((tc, width), F32)
    return pl.pallas_call(
        _rwkv_kernel,
        grid=(batch, nt),
        in_specs=[
            pl.BlockSpec((tc, ncols), lambda b, i: (b * nt + i, 0)),
            pl.BlockSpec((1, ncols), const),
            vec, lowmat, vec, lowmat, lowmat, vec, vec, vec, vec, vec,
        ],
        out_specs=pl.BlockSpec((tc, width), lambda b, i: (b * nt + i, 0)),
        out_shape=jax.ShapeDtypeStruct((n, width), F32),
        scratch_shapes=[
            pltpu.VMEM((1, ncols), F32),
            pltpu.VMEM((width // gl, gl, gl), F32),
            tile, tile, tile, tile, tile, tile, tile, tile, tile,
            pltpu.VMEM((nblk, gl, gl), BF16),
            pltpu.VMEM((nblk, gl, gl), F32),
            pltpu.VMEM((nblk, gl, LANES), F32),
        ],
        compiler_params=pltpu.CompilerParams(
            dimension_semantics=("parallel", "arbitrary"), vmem_limit_bytes=VMEM_LIMIT),
        name="rwkv7",
    )(p_r, mu, w0, wup, a0, aup, gup, k_k, k_a, r_k, ln_g, ln_b)


def _mlstm_kernel(p_ref, cw_ref, cb_ref, gb_ref, lng_ref, o_ref, xbuf, c_ref, n_ref, m_ref, qk_s):
    tcm = p_ref.shape[0]
    NH, DH, L, KC = MLSTM_HEADS, MLSTM_HEAD_DIM, MLSTM_CHUNK, MLSTM_CONV
    wd = NH * DH
    i = pl.program_id(1)

    @pl.when(i == 0)
    def _():
        xbuf[0:SUBLANES, :] = jnp.zeros((SUBLANES, 2 * wd), F32)
        c_ref[...] = jnp.zeros_like(c_ref)
        n_ref[...] = jnp.zeros_like(n_ref)
        m_ref[...] = jnp.full(m_ref.shape, -jnp.inf, F32)

    @pl.when(i > 0)
    def _():
        xbuf[0:SUBLANES, :] = xbuf[tcm:tcm + SUBLANES, :]

    xbuf[SUBLANES:SUBLANES + tcm, :] = p_ref[:, 0:2 * wd]
    acc = cb_ref[...] + cw_ref[KC - 1:KC, :] * xbuf[SUBLANES:SUBLANES + tcm, :]
    for j in range(KC - 1):
        off = SUBLANES - (KC - 1) + j
        acc = acc + cw_ref[j:j + 1, :] * xbuf[off:off + tcm, :]
    qk_s[...] = acc * jax.nn.sigmoid(acc)

    causal = (lax.broadcasted_iota(jnp.int32, (L, L), 1) <= lax.broadcasted_iota(jnp.int32, (L, L), 0))
    tri = jnp.where(causal, 1.0, 0.0).astype(BF16)
    scale = DH ** -0.5

    def chunk(c, carry):
        sl = pl.ds(pl.multiple_of(c * L, L), L)
        gates = p_ref[sl, 4 * wd:4 * wd + LANES] + gb_ref[...]
        log_f = -_softplus(-gates)
        b_col = _exact_lhs_dot(tri, log_f, 3)
        b_row = b_col.T
        li_row = gates.T
        for h in range(NH):
            hs = slice(h * DH, (h + 1) * DH)
            q = qk_s[sl, hs] * scale
            kh = qk_s[sl, wd + h * DH:wd + (h + 1) * DH]
            vh = p_ref[sl, 2 * wd + h * DH:2 * wd + (h + 1) * DH]
            bc = b_col[:, NH + h:NH + h + 1]
            br = b_row[NH + h:NH + h + 1, :]
            lic = gates[:, h:h + 1]
            lir = li_row[h:h + 1, :]
            b_end = bc[L - 1:L, :]
            m_prev = m_ref[h][:, 0:1]
            c_prev = c_ref[h]
            n_prev = n_ref[h]

            d = jnp.where(causal, bc - br + lir, -jnp.inf)
            m_inter = bc + m_prev
            m_t = jnp.maximum(m_inter, jnp.max(d, axis=1, keepdims=True))
            scores = _bdot_nt(q, kh) * jnp.exp(d - m_t)
            inter = jnp.exp(m_inter - m_t)
            num = _bdot(scores, vh) + inter * _bdot(q, c_prev)
            den = jnp.sum(scores, axis=1, keepdims=True) + inter * jnp.sum(q * n_prev, axis=1, keepdims=True)
            hh = num / jnp.maximum(jnp.abs(den), jnp.exp(-m_t))

            g_end = b_end - bc + lic
            m_new = jnp.maximum(b_end + m_prev, jnp.max(g_end, axis=0, keepdims=True))
            keep = jnp.exp(b_end + m_prev - m_new)
            kw = kh * jnp.exp(g_end - m_new)
            c_ref[h] = keep * c_prev + _bdot_tn(kw, vh)
            n_ref[h] = keep * n_prev + jnp.sum(kw, axis=0, keepdims=True)
            m_ref[h] = jnp.broadcast_to(m_new, (1, LANES))

            hc = hh - jnp.mean(hh, axis=1, keepdims=True)
            hn = hc * lax.rsqrt(jnp.mean(hc * hc, axis=1, keepdims=True) + MLSTM_NORM_EPS)
            og = jax.nn.sigmoid(p_ref[sl, 3 * wd + h * DH:3 * wd + (h + 1) * DH])
            o_ref[sl, hs] = og * (hn * lng_ref[:, hs])
        return carry

    lax.fori_loop(0, tcm // L, chunk, 0)


def _mlstm_call(p_m, conv_w, conv_b, gate_b, ln_g, *, batch, seq, tcm):
    n, ncols = p_m.shape
    nh, dh = MLSTM_HEADS, MLSTM_HEAD_DIM
    wd = nh * dh
    nt = seq // tcm
    const = lambda b, i: (0, 0)
    return pl.pallas_call(
        _mlstm_kernel,
        grid=(batch, nt),
        in_specs=[
            pl.BlockSpec((tcm, ncols), lambda b, i: (b * nt + i, 0)),
            pl.BlockSpec((MLSTM_CONV, 2 * wd), const),
            pl.BlockSpec((1, 2 * wd), const),
            pl.BlockSpec((1, LANES), const),
            pl.BlockSpec((1, wd), const),
        ],
        out_specs=pl.BlockSpec((tcm, wd), lambda b, i: (b * nt + i, 0)),
        out_shape=jax.ShapeDtypeStruct((n, wd), F32),
        scratch_shapes=[
            pltpu.VMEM((tcm + SUBLANES, 2 * wd), F32),
            pltpu.VMEM((nh, dh, dh), F32),
            pltpu.VMEM((nh, 1, dh), F32),
            pltpu.VMEM((nh, 1, LANES), F32),
            pltpu.VMEM((tcm, 2 * wd), F32),
        ],
        compiler_params=pltpu.CompilerParams(
            dimension_semantics=("parallel", "arbitrary"), vmem_limit_bytes=VMEM_LIMIT),
        name="mlstm",
    )(p_m, conv_w, conv_b, gate_b, ln_g)


def _mixout_kernel(x_ref, yr_ref, ym_ref, pg_ref, mod_ref, wr_ref, wm_ref, wo_ref, o_ref):
    dm = x_ref.shape[1]
    zr = jnp.dot(yr_ref[...].astype(BF16), wr_ref[...], preferred_element_type=F32)
    zm = jnp.dot(ym_ref[...].astype(BF16), wm_ref[...], preferred_element_type=F32)
    mix = jax.nn.sigmoid(pg_ref[:, 0:dm]) * zr + jax.nn.sigmoid(pg_ref[:, dm:2 * dm]) * zm
    out = jnp.dot(mix.astype(BF16), wo_ref[...], preferred_element_type=F32)
    o_ref[...] = x_ref[...] + (1.0 + mod_ref[0, 5:6, :]) * out


def _mixout_call(x2, y_r, y_m, p_g, mod_l, w_r, w_m, w_o, *, seq, tm):
    n, dm = x2.shape
    const = lambda i: (0, 0)
    rowblk = lambda cols: pl.BlockSpec((tm, cols), lambda i: (i, 0))
    return pl.pallas_call(
        _mixout_kernel,
        grid=(n // tm,),
        in_specs=[
            rowblk(dm), rowblk(y_r.shape[1]), rowblk(y_m.shape[1]), rowblk(p_g.shape[1]),
            pl.BlockSpec((1, 9, dm), lambda i: ((i * tm) // seq, 0, 0)),
            pl.BlockSpec(w_r.shape, const), pl.BlockSpec(w_m.shape, const), pl.BlockSpec(w_o.shape, const),
        ],
        out_specs=rowblk(dm),
        out_shape=jax.ShapeDtypeStruct((n, dm), F32),
        compiler_params=pltpu.CompilerParams(
            dimension_semantics=("parallel",), vmem_limit_bytes=VMEM_LIMIT),
        name="mix_out",
    )(x2, y_r, y_m, p_g, mod_l, w_r, w_m, w_o)


def _pad_cols(a, n):
    return jnp.pad(a, ((0, 0), (0, n - a.shape[1])))


def _row(a, n=None):
    a = a.reshape(1, -1)
    return a if n is None else _pad_cols(a, n)


def _lowrank_pad(mat, row0):
    out = jnp.zeros((RWKV_LOWRANK_PAD, mat.shape[1]), F32)
    return lax.dynamic_update_slice(out, mat, (row0, 0)).astype(BF16)


def kernel(x, c, ada_w, ada_b, norm_g, ffn_w_in, ffn_w_out, mix_w_in, rwkv_mu, rwkv_w0, rwkv_w_up, rwkv_a0, rwkv_a_up, rwkv_g_up, rwkv_k_k, rwkv_k_a, rwkv_r_k, rwkv_ln_g, rwkv_ln_b, mlstm_conv_w, mlstm_conv_b, mlstm_gate_b, mlstm_ln_g, branch_w_rwkv, branch_w_mlstm, mix_w_out, final_g):
    batch, seq, dm = x.shape
    depth = ada_w.shape[0]
    n = batch * seq
    rw = RWKV_HEADS * RWKV_HEAD_DIM
    mw = MLSTM_HEADS * MLSTM_HEAD_DIM
    rwkv_cols = 3 * rw + RWKV_DECAY_RANK + RWKV_A_RANK + RWKV_GATE_RANK
    mlstm_cols = 4 * mw + 2 * MLSTM_HEADS
    rwkv_pad = 3 * rw + RWKV_LOWRANK_PAD
    mlstm_pad = 4 * mw + LANES

    tm_ffn = min(1024, seq)
    tm_mix = min(256, seq)
    tc_rwkv = min(512, seq)
    tc_mlstm = min(512, seq)

    c_pad = jnp.pad(c, ((0, SUBLANES - batch % SUBLANES if batch % SUBLANES else 0), (0, 0)))
    mod = _ada_call(c_pad, ada_w, ada_b)[:, :batch].reshape(depth, batch, 9, dm)

    ffn_w_in_bf = ffn_w_in.astype(BF16)
    ffn_w_out_bf = ffn_w_out.astype(BF16)
    final_row = _row(final_g)

    x2 = x.reshape(n, dm)
    for l in range(depth):
        mod_l = mod[l]
        x2 = _ffn_call(x2, mod_l, _row(norm_g[l, 0]), ffn_w_in_bf[l, 0], ffn_w_out_bf[l, 0], final_row,
                       seq=seq, mod_row=0, final_norm=False, tm=tm_ffn, tf=256)

        w_in = mix_w_in[l]
        w_r = _pad_cols(w_in[:, :rwkv_cols], rwkv_pad).astype(BF16)
        w_m = _pad_cols(w_in[:, rwkv_cols:rwkv_cols + mlstm_cols], mlstm_pad).astype(BF16)
        w_g = w_in[:, rwkv_cols + mlstm_cols:].astype(BF16)
        p_r, p_m, p_g = _mixin_call(x2, mod_l, _row(norm_g[l, 1]), w_r, w_m, w_g, seq=seq, tm=tm_mix)

        y_r = _rwkv_call(
            p_r, _row(rwkv_mu[l], rwkv_pad), _row(rwkv_w0[l]),
            _lowrank_pad(rwkv_w_up[l], 0), _row(rwkv_a0[l]),
            _lowrank_pad(rwkv_a_up[l], RWKV_DECAY_RANK),
            _lowrank_pad(rwkv_g_up[l], RWKV_DECAY_RANK + RWKV_A_RANK),
            _row(rwkv_k_k[l]), _row(rwkv_k_a[l]), _row(rwkv_r_k[l]), _row(rwkv_ln_g[l]), _row(rwkv_ln_b[l]),
            batch=batch, seq=seq, tc=tc_rwkv)
        y_m = _mlstm_call(p_m, mlstm_conv_w[l], _row(mlstm_conv_b[l]), _row(mlstm_gate_b[l], LANES),
                          _row(mlstm_ln_g[l]), batch=batch, seq=seq, tcm=tc_mlstm)

        x2 = _mixout_call(x2, y_r, y_m, p_g, mod_l, branch_w_rwkv[l].astype(BF16),
                          branch_w_mlstm[l].astype(BF16), mix_w_out[l].astype(BF16), seq=seq, tm=512)

        x2 = _ffn_call(x2, mod_l, _row(norm_g[l, 2]), ffn_w_in_bf[l, 1], ffn_w_out_bf[l, 1], final_row,
                       seq=seq, mod_row=6, final_norm=(l == depth - 1), tm=tm_ffn, tf=256)
    return x2.reshape(batch, seq, dm)
```

```python
import functools

import jax
import jax.numpy as jnp
from jax import lax
from jax.experimental import pallas as pl
from jax.experimental.pallas import tpu as pltpu

F32 = jnp.float32
BF16 = jnp.bfloat16

RWKV_HEADS = 8
RWKV_HEAD_DIM = 64
RWKV_DECAY_RANK = 32
RWKV_A_RANK = 32
RWKV_GATE_RANK = 96
MLSTM_HEADS = 4
MLSTM_HEAD_DIM = 128
MLSTM_CONV = 4
MLSTM_CHUNK = 128
RMS_EPS = 1e-6
RWKV_GN_EPS = 64e-5
MLSTM_NORM_EPS = 1e-5
L2_EPS = 1e-12

LANES = 128
SUBLANES = 8
VMEM_LIMIT = 56 * 1024 * 1024

RWKV_CHUNK = 64
RWKV_GROUP_HEADS = 4
RWKV_PREP_UNROLL = 4
RWKV_LOWRANK_PAD = 256


def _bdot(a, b):
    return jnp.dot(a.astype(BF16), b.astype(BF16), preferred_element_type=F32)


def _bdot_nt(a, b):
    return lax.dot_general(a.astype(BF16), b.astype(BF16), (((1,), (1,)), ((), ())),
                           preferred_element_type=F32)


def _bdot_tn(a, b):
    return lax.dot_general(a.astype(BF16), b.astype(BF16), (((0,), (0,)), ((), ())),
                           preferred_element_type=F32)


def _split_bf16(x, parts):
    out = []
    rem = x
    for _ in range(parts):
        p = rem.astype(BF16)
        out.append(p)
        rem = rem - p.astype(F32)
    return out


def _dot_exact_rhs(x, m_bf16, parts):
    acc = None
    for p in _split_bf16(x, parts):
        t = jnp.dot(p, m_bf16, preferred_element_type=F32)
        acc = t if acc is None else acc + t
    return acc


def _exact_lhs_dot(m_bf16, x, parts):
    acc = None
    for p in _split_bf16(x, parts):
        t = jnp.dot(m_bf16, p, preferred_element_type=F32)
        acc = t if acc is None else acc + t
    return acc


def _rms_mod(x, g, shift, scale):
    y = x * lax.rsqrt(jnp.mean(x * x, axis=-1, keepdims=True) + RMS_EPS) * g
    return y * (1.0 + scale) + shift


def _softplus(z):
    return jnp.maximum(z, 0.0) + jnp.log1p(jnp.exp(-jnp.abs(z)))


def _ada_kernel(c_ref, w_ref, b_ref, o_ref):
    c = c_ref[...]
    cond = c * jax.nn.sigmoid(c)
    o_ref[0] = _bdot(cond, w_ref[0]) + b_ref[0]


def _ada_call(c_pad, ada_w, ada_b):
    depth, dm, n9 = ada_w.shape
    tn = n9 // 4
    rows = c_pad.shape[0]
    return pl.pallas_call(
        _ada_kernel,
        grid=(depth, n9 // tn),
        in_specs=[
            pl.BlockSpec((rows, dm), lambda l, j: (0, 0)),
            pl.BlockSpec((1, dm, tn), lambda l, j: (l, 0, j)),
            pl.BlockSpec((1, 1, tn), lambda l, j: (l, 0, j)),
        ],
        out_specs=pl.BlockSpec((1, rows, tn), lambda l, j: (l, 0, j)),
        out_shape=jax.ShapeDtypeStruct((depth, rows, n9), F32),
        compiler_params=pltpu.CompilerParams(
            dimension_semantics=("arbitrary", "arbitrary"), vmem_limit_bytes=VMEM_LIMIT),
        name="ada_mod",
    )(c_pad, ada_w, ada_b.reshape(depth, 1, n9))


def _ffn_kernel(x_ref, mod_ref, g_ref, wi_ref, wo_ref, fg_ref, o_ref, *, mod_row, final_norm):
    dff = wo_ref.shape[0]
    x = x_ref[...]
    h = _rms_mod(x, g_ref[...], mod_ref[0, mod_row:mod_row + 1, :], mod_ref[0, mod_row + 1:mod_row + 2, :])
    gu = jnp.dot(h.astype(BF16), wi_ref[...], preferred_element_type=F32)
    gate = gu[:, :dff]
    act = (gate * jax.nn.sigmoid(gate)) * gu[:, dff:]
    y = jnp.dot(act.astype(BF16), wo_ref[...], preferred_element_type=F32)
    out = x + (0.5 * (1.0 + mod_ref[0, mod_row + 2:mod_row + 3, :])) * y
    if final_norm:
        out = out * lax.rsqrt(jnp.mean(out * out, axis=-1, keepdims=True) + RMS_EPS) * fg_ref[...]
    o_ref[...] = out


def _resident(shape):
    return pl.BlockSpec(shape, lambda *_: (0,) * len(shape), pipeline_mode=pl.Buffered(1))


def _ffn_call(x2, mod_l, norm_g, w_in, w_out, final_g, *, seq, mod_row, final_norm, tm):
    n, dm = x2.shape
    kern = functools.partial(_ffn_kernel, mod_row=mod_row, final_norm=final_norm)
    return pl.pallas_call(
        kern,
        grid=(n // tm,),
        in_specs=[
            pl.BlockSpec((tm, dm), lambda i: (i, 0)),
            pl.BlockSpec((1, 9, dm), lambda i: ((i * tm) // seq, 0, 0)),
            _resident((1, dm)),
            _resident(w_in.shape),
            _resident(w_out.shape),
            _resident((1, dm)),
        ],
        out_specs=pl.BlockSpec((tm, dm), lambda i: (i, 0)),
        out_shape=jax.ShapeDtypeStruct((n, dm), F32),
        compiler_params=pltpu.CompilerParams(
            dimension_semantics=("parallel",), vmem_limit_bytes=VMEM_LIMIT),
        name="ffn",
    )(x2, mod_l, norm_g, w_in, w_out, final_g)


def _mixin_kernel(x_ref, mod_ref, g_ref, wr_ref, wm_ref, wg_ref, pr_ref, pm_ref, pg_ref):
    h = _rms_mod(x_ref[...], g_ref[...], mod_ref[0, 3:4, :], mod_ref[0, 4:5, :]).astype(BF16)
    pr_ref[...] = jnp.dot(h, wr_ref[...], preferred_element_type=F32)
    pm_ref[...] = jnp.dot(h, wm_ref[...], preferred_element_type=F32)
    pg_ref[...] = jnp.dot(h, wg_ref[...], preferred_element_type=F32)


def _mixin_call(x2, mod_l, norm_g, w_r, w_m, w_g, *, seq, tm):
    n, dm = x2.shape
    nr, nm, ng = w_r.shape[1], w_m.shape[1], w_g.shape[1]
    const = lambda i: (0, 0)
    return pl.pallas_call(
        _mixin_kernel,
        grid=(n // tm,),
        in_specs=[
            pl.BlockSpec((tm, dm), lambda i: (i, 0)),
            pl.BlockSpec((1, 9, dm), lambda i: ((i * tm) // seq, 0, 0)),
            _resident((1, dm)),
            _resident((dm, nr)),
            _resident((dm, nm)),
            _resident((dm, ng)),
        ],
        out_specs=[
            pl.BlockSpec((tm, nr), lambda i: (i, 0)),
            pl.BlockSpec((tm, nm), lambda i: (i, 0)),
            pl.BlockSpec((tm, ng), lambda i: (i, 0)),
        ],
        out_shape=[jax.ShapeDtypeStruct((n, nr), F32), jax.ShapeDtypeStruct((n, nm), F32),
                   jax.ShapeDtypeStruct((n, ng), F32)],
        compiler_params=pltpu.CompilerParams(
            dimension_semantics=("parallel",), vmem_limit_bytes=VMEM_LIMIT),
        name="mix_in",
    )(x2, mod_l, norm_g, w_r, w_m, w_g)


def _rwkv_kernel(p_ref, mu_ref, w0_ref, wup_ref, a0_ref, aup_ref, gup_ref, kk_ref, ka_ref, rk_ref,
                 lng_ref, lnb_ref, o_ref,
                 prev_ref, s_ref, r_s, ld_s, k_s, v_s, a_s, b_s, y_s, bonus_s, g_s, mx_s, n0_s, pc_s):
    tc = p_ref.shape[0]
    width = o_ref.shape[1]
    C = RWKV_CHUNK
    HD = RWKV_HEAD_DIM
    HG = RWKV_GROUP_HEADS
    GL = HG * HD
    n_groups = width // GL
    shift = HD.bit_length() - 1
    i = pl.program_id(1)

    @pl.when(i == 0)
    def _():
        prev_ref[...] = jnp.zeros_like(prev_ref)
        s_ref[...] = jnp.zeros_like(s_ref)

    pf = p_ref[...]
    rolled = pltpu.roll(pf, 1, axis=0)
    row = lax.broadcasted_iota(jnp.int32, (tc, 1), 0)
    prev = jnp.where(row == 0, prev_ref[...], rolled)
    prev_ref[...] = pf[tc - 1:tc, :]
    xs = pf + (prev - pf) * mu_ref[...]

    r = xs[:, 0:width]
    k = xs[:, width:2 * width]
    v = xs[:, 2 * width:3 * width]
    low = xs[:, 3 * width:3 * width + RWKV_LOWRANK_PAD]

    rg = lax.broadcasted_iota(jnp.int32, (GL, GL), 0)
    cg = lax.broadcasted_iota(jnp.int32, (GL, GL), 1)
    blockmask = (rg >> shift) == (cg >> shift)
    eye = rg == cg
    ones_bd = jnp.where(blockmask, 1.0, 0.0).astype(BF16)

    def segsum(x):
        return jnp.concatenate(
            [_dot_exact_rhs(x[:, g * GL:(g + 1) * GL], ones_bd, 2) for g in range(n_groups)], axis=1)

    w = w0_ref[...] + _bdot(jnp.tanh(low), wup_ref[...])
    w = -_softplus(-w) - 0.5
    ld_s[...] = -jnp.exp(w)
    asig = jax.nn.sigmoid(a0_ref[...] + _bdot(low, aup_ref[...]))
    g_s[...] = _bdot(jax.nn.sigmoid(low), gup_ref[...])
    kk = k * kk_ref[...]
    kk = kk / jnp.maximum(jnp.sqrt(segsum(kk * kk)), L2_EPS)
    k2 = k * (1.0 + (asig - 1.0) * ka_ref[...])
    bonus_s[...] = segsum(r * k2 * rk_ref[...]) * v
    r_s[...] = r
    k_s[...] = k2
    v_s[...] = v
    a_s[...] = -kk
    b_s[...] = kk * asig

    lane = lax.broadcasted_iota(jnp.int32, (C, GL), 1)
    tok = lax.broadcasted_iota(jnp.int32, (C, GL), 0)
    strict = (lane & (HD - 1)) < tok
    incl = (lane & (HD - 1)) <= tok
    headmask = [(lane >> shift) == h for h in range(HG)]
    tri = (lax.broadcasted_iota(jnp.int32, (C, C), 1) <= lax.broadcasted_iota(jnp.int32, (C, C), 0))
    tri = jnp.where(tri, 1.0, 0.0).astype(BF16)

    def bd(x):
        return jnp.where(blockmask, jnp.concatenate([x] * HG, axis=0), 0.0).astype(BF16)

    levels = C.bit_length() - 1

    def prepare_body(c2, carry):
        inst = []
        for u in range(RWKV_PREP_UNROLL):
            c = c2 * RWKV_PREP_UNROLL + u
            sl = pl.ds(pl.multiple_of(c * C, C), C)
            ld = ld_s[sl, :]
            lp = _exact_lhs_dot(tri, ld, 3)
            lpc = lp[C - 1:C, :]
            e_lp = jnp.exp(lp)
            e_nlp = jnp.exp(-lp)
            e_end = jnp.exp(lpc - lp)
            rt = r_s[sl, :] * e_lp
            at = a_s[sl, :] * jnp.exp(lp - ld)
            bb = b_s[sl, :]
            kc = k_s[sl, :]
            bt = bb * e_nlp
            kt = kc * e_nlp
            bh = bb * e_end
            kh = kc * e_end
            vv = v_s[sl, :]
            pc = jnp.exp(lpc)
            for g in range(n_groups):
                gs = slice(g * GL, (g + 1) * GL)
                inst.append(dict(sl=sl, gs=gs, idx=c * n_groups + g, at=at[:, gs], rt=rt[:, gs], v=vv[:, gs],
                                 bt=bt[:, gs], kt=kt[:, gs], bh=bh[:, gs], kh=kh[:, gs], pc=pc[:, gs]))
        for t in inst:
            lhs = jnp.concatenate([t['at'], t['rt']], axis=0)
            rows = [jnp.where(headmask[h], t['bt'], 0.0) for h in range(HG)]
            rows += [jnp.where(headmask[h], t['kt'], 0.0) for h in range(HG)]
            gram = _bdot_nt(lhs, jnp.concatenate(rows, axis=0))
            t['a_ab'] = jnp.where(strict, gram[:C, :GL], 0.0)
            t['a_ak'] = jnp.where(strict, gram[:C, GL:], 0.0)
            t['a_rb'] = jnp.where(incl, gram[C:, :GL], 0.0).astype(BF16)
            t['a_rk'] = jnp.where(incl, gram[C:, GL:], 0.0).astype(BF16)
        for t in inst:
            t['bdv'] = bd(t['v'])
            t['akv'] = _bdot(t['a_ak'], t['bdv'])
            t['nm'] = t['a_ab']
            t['tr'] = t['a_ab']
            t['nbd'] = bd(t['nm'])
        for _ in range(1, levels):
            for t in inst:
                t['nm'] = jnp.dot(t['nm'].astype(BF16), t['nbd'], preferred_element_type=F32)
                t['nbd'] = bd(t['nm'])
            for t in inst:
                t['tr'] = t['tr'] + t['nm'] + _bdot(t['tr'], t['nbd'])
        for t in inst:
            wu = _bdot(t['tr'], jnp.concatenate([bd(t['at']), bd(t['akv'])], axis=1))
            t['w'] = t['at'] + wu[:, :GL]
            t['u0'] = t['akv'] + wu[:, GL:]
        for t in inst:
            ry = jnp.dot(t['a_rb'], jnp.concatenate([bd(t['w']), bd(t['u0'])], axis=1),
                         preferred_element_type=F32)
            r_s[t['sl'], t['gs']] = t['rt'] + ry[:, :GL]
            y_s[t['sl'], t['gs']] = ry[:, GL:] + jnp.dot(t['a_rk'], t['bdv'], preferred_element_type=F32)
        for t in inst:
            mn = _bdot_tn(t['bh'], jnp.concatenate([t['w'], t['u0']], axis=1))
            mx_s[t['idx']] = jnp.where(blockmask, mn[:, :GL], 0.0).astype(BF16)
            n0_s[t['idx']] = jnp.where(blockmask, mn[:, GL:] + _bdot_tn(t['kh'], t['v']), 0.0)
            pc_s[t['idx']] = jnp.broadcast_to(
                jnp.sum(jnp.where(eye, jnp.broadcast_to(t['pc'], (GL, GL)), 0.0), axis=1, keepdims=True),
                (GL, LANES))
        return carry

    lax.fori_loop(0, tc // (C * RWKV_PREP_UNROLL), prepare_body, 0)

    def advance(c, carry):
        sl = pl.ds(pl.multiple_of(c * C, C), C)
        for g in range(n_groups):
            gs = slice(g * GL, (g + 1) * GL)
            idx = c * n_groups + g
            s_old = s_ref[g]
            s_bf = s_old.astype(BF16)
            y_s[sl, gs] = y_s[sl, gs] + jnp.dot(r_s[sl, gs].astype(BF16), s_bf, preferred_element_type=F32)
            s_ref[g] = (pc_s[idx][:, 0:1] * s_old + n0_s[idx]
                        + jnp.dot(mx_s[idx], s_bf, preferred_element_type=F32))
        return carry

    lax.fori_loop(0, tc // C, advance, 0)

    y = y_s[...]
    inv = 1.0 / HD
    yc = y - segsum(y) * inv
    var = segsum(yc * yc) * inv
    yn = yc * lax.rsqrt(var + RWKV_GN_EPS) * lng_ref[...] + lnb_ref[...]
    o_ref[...] = (yn + bonus_s[...]) * g_s[...]


def _rwkv_call(p_r, mu, w0, wup, a0, aup, gup, k_k, k_a, r_k, ln_g, ln_b, *, batch, seq, tc):
    n, ncols = p_r.shape
    width = RWKV_HEADS * RWKV_HEAD_DIM
    gl = RWKV_GROUP_HEADS * RWKV_HEAD_DIM
    nt = seq // tc
    nblk = (tc // RWKV_CHUNK) * (width // gl)
    const = lambda b, i: (0, 0)
    vec = pl.BlockSpec((1, width), const)
    lowmat = pl.BlockSpec((RWKV_LOWRANK_PAD, width), const)
    tile = pltpu.VMEM((tc, width), F32)
    return pl.pallas_call(
        _rwkv_kernel,
        grid=(batch, nt),
        in_specs=[
            pl.BlockSpec((tc, ncols), lambda b, i: (b * nt + i, 0)),
            pl.BlockSpec((1, ncols), const),
            vec, lowmat, vec, lowmat, lowmat, vec, vec, vec, vec, vec,
        ],
        out_specs=pl.BlockSpec((tc, width), lambda b, i: (b * nt + i, 0)),
        out_shape=jax.ShapeDtypeStruct((n, width), F32),
        scratch_shapes=[
            pltpu.VMEM((1, ncols), F32),
            pltpu.VMEM((width // gl, gl, gl), F32),
            tile, tile, tile, tile, tile, tile, tile, tile, tile,
            pltpu.VMEM((nblk, gl, gl), BF16),
            pltpu.VMEM((nblk, gl, gl), F32),
            pltpu.VMEM((nblk, gl, LANES), F32),
        ],
        compiler_params=pltpu.CompilerParams(
            dimension_semantics=("parallel", "arbitrary"), vmem_limit_bytes=VMEM_LIMIT),
        name="rwkv7",
    )(p_r, mu, w0, wup, a0, aup, gup, k_k, k_a, r_k, ln_g, ln_b)


def _cummax_rows(x):
    n = x.shape[0]
    row = lax.broadcasted_iota(jnp.int32, x.shape, 0)
    s = 1
    while s < n:
        x = jnp.maximum(x, jnp.where(row >= s, pltpu.roll(x, s, axis=0), -jnp.inf))
        s *= 2
    return x


def _mlstm_kernel(p_ref, cw_ref, cb_ref, gbi_ref, gbf_ref, lng_ref, o_ref, xbuf, cn_ref, m_ref, qk_s):
    tcm = p_ref.shape[0]
    NH, DH, L, KC = MLSTM_HEADS, MLSTM_HEAD_DIM, MLSTM_CHUNK, MLSTM_CONV
    wd = NH * DH
    i = pl.program_id(1)

    @pl.when(i == 0)
    def _():
        xbuf[0:SUBLANES, :] = jnp.zeros((SUBLANES, 2 * wd), F32)
        cn_ref[...] = jnp.zeros_like(cn_ref)
        m_ref[...] = jnp.full(m_ref.shape, -jnp.inf, F32)

    @pl.when(i > 0)
    def _():
        xbuf[0:SUBLANES, :] = xbuf[tcm:tcm + SUBLANES, :]

    xbuf[SUBLANES:SUBLANES + tcm, :] = p_ref[:, 0:2 * wd]
    acc = cb_ref[...] + cw_ref[KC - 1:KC, :] * xbuf[SUBLANES:SUBLANES + tcm, :]
    for j in range(KC - 1):
        off = SUBLANES - (KC - 1) + j
        acc = acc + cw_ref[j:j + 1, :] * xbuf[off:off + tcm, :]
    qk_s[...] = acc * jax.nn.sigmoid(acc)

    rr = lax.broadcasted_iota(jnp.int32, (L, L), 0)
    cc = lax.broadcasted_iota(jnp.int32, (L, L), 1)
    causal = cc <= rr
    tri = jnp.where(causal, 1.0, 0.0).astype(BF16)
    eye_l = jnp.where(cc == rr, 1.0, 0.0).astype(BF16)
    eye_h = eye_l[0:2 * SUBLANES, :]
    ones_v = jnp.ones((L, DH), BF16)
    scale = DH ** -0.5
    heads = range(NH)

    def chunk(c, carry):
        sl = pl.ds(pl.multiple_of(c * L, L), L)
        li = p_ref[sl, 4 * wd:4 * wd + LANES] + gbi_ref[...]
        lf = -_softplus(-(p_ref[sl, 4 * wd + LANES:4 * wd + 2 * LANES] + gbf_ref[...]))
        b_col = _exact_lhs_dot(tri, lf, 3)
        xcol = li - b_col
        m_prev = m_ref[...]
        zcol = jnp.maximum(m_prev, _cummax_rows(xcol))
        b_end = b_col[L - 1:L, :]
        m_new = jnp.maximum(b_end + m_prev, jnp.max(b_end + xcol, axis=0, keepdims=True))
        keep = jnp.exp(b_end + m_prev - m_new)
        m_ref[...] = m_new
        wshift = b_end - m_new
        xrow = None
        for part in _split_bf16(xcol, 3):
            t = lax.dot_general(eye_h, part, (((1,), (1,)), ((), ())), preferred_element_type=F32)
            xrow = t if xrow is None else xrow + t

        q = [(qk_s[sl, h * DH:(h + 1) * DH] * scale).astype(BF16) for h in heads]
        k = [qk_s[sl, wd + h * DH:wd + (h + 1) * DH].astype(BF16) for h in heads]
        vx = [jnp.concatenate([p_ref[sl, 2 * wd + h * DH:2 * wd + (h + 1) * DH].astype(BF16), ones_v], axis=1)
              for h in heads]
        cn = [cn_ref[h] for h in heads]
        s = [lax.dot_general(q[h], k[h], (((1,), (1,)), ((), ())), preferred_element_type=F32) for h in heads]
        kt = [lax.dot_general(eye_l, k[h], (((1,), (1,)), ((), ())), preferred_element_type=F32) for h in heads]
        qc = [jnp.dot(q[h], cn[h].astype(BF16), preferred_element_type=F32) for h in heads]
        zb = [jnp.broadcast_to(zcol[:, h:h + 1], (L, DH)) for h in heads]
        bb = [jnp.broadcast_to(b_col[:, h:h + 1], (L, DH)) for h in heads]
        pw = [(s[h] * jnp.exp(jnp.where(causal, xrow[h:h + 1, :] - zb[h], -jnp.inf))).astype(BF16) for h in heads]
        nd = [jnp.dot(pw[h], vx[h], preferred_element_type=F32) for h in heads]
        for h in heads:
            wrow = jnp.exp(xrow[h:h + 1, :] + wshift[:, h:h + 1])
            cn_ref[h] = keep[:, h:h + 1] * cn[h] + jnp.dot((kt[h] * wrow).astype(BF16), vx[h],
                                                            preferred_element_type=F32)
        for h in heads:
            hs = slice(h * DH, (h + 1) * DH)
            inter = jnp.exp(m_prev[:, h:h + 1] - zb[h])
            num = nd[h][:, :DH] + inter * qc[h][:, :DH]
            den = nd[h][:, DH:] + inter * qc[h][:, DH:]
            hh = num / jnp.maximum(jnp.abs(den), jnp.exp(-(bb[h] + zb[h])))
            hc = hh - jnp.mean(hh, axis=1, keepdims=True)
            hn = hc * lax.rsqrt(jnp.mean(hc * hc, axis=1, keepdims=True) + MLSTM_NORM_EPS)
            og = jax.nn.sigmoid(p_ref[sl, 3 * wd + h * DH:3 * wd + (h + 1) * DH])
            o_ref[sl, hs] = og * (hn * lng_ref[:, hs])
        return carry

    lax.fori_loop(0, tcm // L, chunk, 0)


def _mlstm_call(p_m, conv_w, conv_b, gate_bi, gate_bf, ln_g, *, batch, seq, tcm):
    n, ncols = p_m.shape
    nh, dh = MLSTM_HEADS, MLSTM_HEAD_DIM
    wd = nh * dh
    nt = seq // tcm
    return pl.pallas_call(
        _mlstm_kernel,
        grid=(batch, nt),
        in_specs=[
            pl.BlockSpec((tcm, ncols), lambda b, i: (b * nt + i, 0)),
            _resident((MLSTM_CONV, 2 * wd)),
            _resident((1, 2 * wd)),
            _resident((1, LANES)),
            _resident((1, LANES)),
            _resident((1, wd)),
        ],
        out_specs=pl.BlockSpec((tcm, wd), lambda b, i: (b * nt + i, 0)),
        out_shape=jax.ShapeDtypeStruct((n, wd), F32),
        scratch_shapes=[
            pltpu.VMEM((tcm + SUBLANES, 2 * wd), F32),
            pltpu.VMEM((nh, dh, 2 * dh), F32),
            pltpu.VMEM((1, LANES), F32),
            pltpu.VMEM((tcm, 2 * wd), F32),
        ],
        compiler_params=pltpu.CompilerParams(
            dimension_semantics=("parallel", "arbitrary"), vmem_limit_bytes=VMEM_LIMIT),
        name="mlstm",
    )(p_m, conv_w, conv_b, gate_bi, gate_bf, ln_g)


def _mixout_kernel(x_ref, yr_ref, ym_ref, pg_ref, mod_ref, wr_ref, wm_ref, wo_ref, o_ref):
    dm = x_ref.shape[1]
    zr = jnp.dot(yr_ref[...].astype(BF16), wr_ref[...], preferred_element_type=F32)
    zm = jnp.dot(ym_ref[...].astype(BF16), wm_ref[...], preferred_element_type=F32)
    mix = jax.nn.sigmoid(pg_ref[:, 0:dm]) * zr + jax.nn.sigmoid(pg_ref[:, dm:2 * dm]) * zm
    out = jnp.dot(mix.astype(BF16), wo_ref[...], preferred_element_type=F32)
    o_ref[...] = x_ref[...] + (1.0 + mod_ref[0, 5:6, :]) * out


def _mixout_call(x2, y_r, y_m, p_g, mod_l, w_r, w_m, w_o, *, seq, tm):
    n, dm = x2.shape
    const = lambda i: (0, 0)
    rowblk = lambda cols: pl.BlockSpec((tm, cols), lambda i: (i, 0))
    return pl.pallas_call(
        _mixout_kernel,
        grid=(n // tm,),
        in_specs=[
            rowblk(dm), rowblk(y_r.shape[1]), rowblk(y_m.shape[1]), rowblk(p_g.shape[1]),
            pl.BlockSpec((1, 9, dm), lambda i: ((i * tm) // seq, 0, 0)),
            _resident(w_r.shape), _resident(w_m.shape), _resident(w_o.shape),
        ],
        out_specs=rowblk(dm),
        out_shape=jax.ShapeDtypeStruct((n, dm), F32),
        compiler_params=pltpu.CompilerParams(
            dimension_semantics=("parallel",), vmem_limit_bytes=VMEM_LIMIT),
        name="mix_out",
    )(x2, y_r, y_m, p_g, mod_l, w_r, w_m, w_o)


def _pad_cols(a, n):
    return jnp.pad(a, ((0, 0), (0, n - a.shape[1])))


def _row(a, n=None):
    a = a.reshape(1, -1)
    return a if n is None else _pad_cols(a, n)


def _lowrank_pad(mat, row0):
    out = jnp.zeros((RWKV_LOWRANK_PAD, mat.shape[1]), F32)
    return lax.dynamic_update_slice(out, mat, (row0, 0)).astype(BF16)


def kernel(x, c, ada_w, ada_b, norm_g, ffn_w_in, ffn_w_out, mix_w_in, rwkv_mu, rwkv_w0, rwkv_w_up, rwkv_a0, rwkv_a_up, rwkv_g_up, rwkv_k_k, rwkv_k_a, rwkv_r_k, rwkv_ln_g, rwkv_ln_b, mlstm_conv_w, mlstm_conv_b, mlstm_gate_b, mlstm_ln_g, branch_w_rwkv, branch_w_mlstm, mix_w_out, final_g):
    batch, seq, dm = x.shape
    depth = ada_w.shape[0]
    n = batch * seq
    rw = RWKV_HEADS * RWKV_HEAD_DIM
    mw = MLSTM_HEADS * MLSTM_HEAD_DIM
    rwkv_cols = 3 * rw + RWKV_DECAY_RANK + RWKV_A_RANK + RWKV_GATE_RANK
    mlstm_cols = 4 * mw + 2 * MLSTM_HEADS
    rwkv_pad = 3 * rw + RWKV_LOWRANK_PAD
    mlstm_pad = 4 * mw + LANES

    tm_ffn = min(512, seq)
    tm_mix = min(512, seq)
    tc_rwkv = min(512, seq)
    tc_mlstm = min(512, seq)

    c_pad = jnp.pad(c, ((0, SUBLANES - batch % SUBLANES if batch % SUBLANES else 0), (0, 0)))
    mod = _ada_call(c_pad, ada_w, ada_b)[:, :batch].reshape(depth, batch, 9, dm)

    ffn_w_in_bf = ffn_w_in.astype(BF16)
    ffn_w_out_bf = ffn_w_out.astype(BF16)
    final_row = _row(final_g)

    x2 = x.reshape(n, dm)
    for l in range(depth):
        mod_l = mod[l]
        x2 = _ffn_call(x2, mod_l, _row(norm_g[l, 0]), ffn_w_in_bf[l, 0], ffn_w_out_bf[l, 0], final_row,
                       seq=seq, mod_row=0, final_norm=False, tm=tm_ffn)

        w_in = mix_w_in[l]
        w_r = _pad_cols(w_in[:, :rwkv_cols], rwkv_pad).astype(BF16)
        gate0 = rwkv_cols + 4 * mw
        w_m = jnp.concatenate([
            w_in[:, rwkv_cols:gate0],
            _pad_cols(w_in[:, gate0:gate0 + MLSTM_HEADS], LANES),
            _pad_cols(w_in[:, gate0 + MLSTM_HEADS:gate0 + 2 * MLSTM_HEADS], LANES)], axis=1).astype(BF16)
        w_g = w_in[:, rwkv_cols + mlstm_cols:].astype(BF16)
        p_r, p_m, p_g = _mixin_call(x2, mod_l, _row(norm_g[l, 1]), w_r, w_m, w_g, seq=seq, tm=tm_mix)

        y_r = _rwkv_call(
            p_r, _row(rwkv_mu[l], rwkv_pad), _row(rwkv_w0[l]),
            _lowrank_pad(rwkv_w_up[l], 0), _row(rwkv_a0[l]),
            _lowrank_pad(rwkv_a_up[l], RWKV_DECAY_RANK),
            _lowrank_pad(rwkv_g_up[l], RWKV_DECAY_RANK + RWKV_A_RANK),
            _row(rwkv_k_k[l]), _row(rwkv_k_a[l]), _row(rwkv_r_k[l]), _row(rwkv_ln_g[l]), _row(rwkv_ln_b[l]),
            batch=batch, seq=seq, tc=tc_rwkv)
        y_m = _mlstm_call(p_m, mlstm_conv_w[l], _row(mlstm_conv_b[l]),
                          _row(mlstm_gate_b[l, :MLSTM_HEADS], LANES), _row(mlstm_gate_b[l, MLSTM_HEADS:], LANES),
                          _row(mlstm_ln_g[l]), batch=batch, seq=seq, tcm=tc_mlstm)

        x2 = _mixout_call(x2, y_r, y_m, p_g, mod_l, branch_w_rwkv[l].astype(BF16),
                          branch_w_mlstm[l].astype(BF16), mix_w_out[l].astype(BF16), seq=seq, tm=512)

        x2 = _ffn_call(x2, mod_l, _row(norm_g[l, 2]), ffn_w_in_bf[l, 1], ffn_w_out_bf[l, 1], final_row,
                       seq=seq, mod_row=6, final_norm=(l == depth - 1), tm=tm_ffn)
    return x2.reshape(batch, seq, dm)
```

```python
import functools
import math

import jax
import jax.numpy as jnp
from jax import lax
from jax.experimental import pallas as pl
from jax.experimental.pallas import tpu as pltpu

F32 = jnp.float32
BF16 = jnp.bfloat16

RWKV_HEADS = 8
RWKV_HEAD_DIM = 64
RWKV_DECAY_RANK = 32
RWKV_A_RANK = 32
RWKV_GATE_RANK = 96
MLSTM_HEADS = 4
MLSTM_HEAD_DIM = 128
MLSTM_CONV = 4
MLSTM_CHUNK = 128
RMS_EPS = 1e-6
RWKV_GN_EPS = 64e-5
MLSTM_NORM_EPS = 1e-5
L2_EPS = 1e-12

LANES = 128
SUBLANES = 8
VMEM_LIMIT = 56 * 1024 * 1024

RWKV_CHUNK = 64
RWKV_GROUP_HEADS = 2
RWKV_PREP_UNROLL = 4
RWKV_LOWRANK_PAD = 256


def _bdot(a, b):
    return jnp.dot(a.astype(BF16), b.astype(BF16), preferred_element_type=F32)


def _bdot_nt(a, b):
    return lax.dot_general(a.astype(BF16), b.astype(BF16), (((1,), (1,)), ((), ())),
                           preferred_element_type=F32)


def _bdot_tn(a, b):
    return lax.dot_general(a.astype(BF16), b.astype(BF16), (((0,), (0,)), ((), ())),
                           preferred_element_type=F32)


def _split_bf16(x, parts):
    out = []
    rem = x
    for _ in range(parts):
        p = rem.astype(BF16)
        out.append(p)
        rem = rem - p.astype(F32)
    return out


def _dot_exact_rhs(x, m_bf16, parts):
    acc = None
    for p in _split_bf16(x, parts):
        t = jnp.dot(p, m_bf16, preferred_element_type=F32)
        acc = t if acc is None else acc + t
    return acc


def _exact_lhs_dot(m_bf16, x, parts):
    acc = None
    for p in _split_bf16(x, parts):
        t = jnp.dot(m_bf16, p, preferred_element_type=F32)
        acc = t if acc is None else acc + t
    return acc


def _rms_mod(x, g, shift, scale):
    y = x * lax.rsqrt(jnp.mean(x * x, axis=-1, keepdims=True) + RMS_EPS) * g
    return y * (1.0 + scale) + shift


def _softplus(z):
    return jnp.maximum(z, 0.0) + jnp.log1p(jnp.exp(-jnp.abs(z)))


def _ada_kernel(c_ref, w_ref, b_ref, o_ref):
    c = c_ref[...]
    cond = c * jax.nn.sigmoid(c)
    o_ref[0] = _bdot(cond, w_ref[0]) + b_ref[0]


def _ada_call(c_pad, ada_w, ada_b):
    depth, dm, n9 = ada_w.shape
    tn = n9 // 4
    rows = c_pad.shape[0]
    return pl.pallas_call(
        _ada_kernel,
        grid=(depth, n9 // tn),
        in_specs=[
            pl.BlockSpec((rows, dm), lambda l, j: (0, 0)),
            pl.BlockSpec((1, dm, tn), lambda l, j: (l, 0, j)),
            pl.BlockSpec((1, 1, tn), lambda l, j: (l, 0, j)),
        ],
        out_specs=pl.BlockSpec((1, rows, tn), lambda l, j: (l, 0, j)),
        out_shape=jax.ShapeDtypeStruct((depth, rows, n9), F32),
        compiler_params=pltpu.CompilerParams(
            dimension_semantics=("arbitrary", "arbitrary"), vmem_limit_bytes=VMEM_LIMIT),
        name="ada_mod",
    )(c_pad, ada_w, ada_b.reshape(depth, 1, n9))


def _ffn_half_step(x, mod_ref, g_ref, wi_ref, wo_ref, fg_ref, *, mod_row, final_norm):
    dff = wo_ref.shape[0]
    h = _rms_mod(x, g_ref[...], mod_ref[0, mod_row:mod_row + 1, :], mod_ref[0, mod_row + 1:mod_row + 2, :])
    gu = jnp.dot(h.astype(BF16), wi_ref[...], preferred_element_type=F32)
    gate = gu[:, :dff]
    act = (gate * jax.nn.sigmoid(gate)) * gu[:, dff:]
    y = jnp.dot(act.astype(BF16), wo_ref[...], preferred_element_type=F32)
    out = x + (0.5 * (1.0 + mod_ref[0, mod_row + 2:mod_row + 3, :])) * y
    if final_norm:
        out = out * lax.rsqrt(jnp.mean(out * out, axis=-1, keepdims=True) + RMS_EPS) * fg_ref[...]
    return out


def _ffn_kernel(x_ref, mod_ref, g_ref, wi_ref, wo_ref, fg_ref, o_ref, *, mod_row, final_norm):
    o_ref[...] = _ffn_half_step(x_ref[...], mod_ref, g_ref, wi_ref, wo_ref, fg_ref,
                                mod_row=mod_row, final_norm=final_norm)


def _resident(shape):
    return pl.BlockSpec(shape, lambda *_: (0,) * len(shape), pipeline_mode=pl.Buffered(1))


def _ffn_call(x2, mod_l, norm_g, w_in, w_out, final_g, *, seq, mod_row, final_norm, tm):
    n, dm = x2.shape
    kern = functools.partial(_ffn_kernel, mod_row=mod_row, final_norm=final_norm)
    return pl.pallas_call(
        kern,
        grid=(n // tm,),
        in_specs=[
            pl.BlockSpec((tm, dm), lambda i: (i, 0)),
            pl.BlockSpec((1, 9, dm), lambda i: ((i * tm) // seq, 0, 0)),
            _resident((1, dm)),
            _resident(w_in.shape),
            _resident(w_out.shape),
            _resident((1, dm)),
        ],
        out_specs=pl.BlockSpec((tm, dm), lambda i: (i, 0)),
        out_shape=jax.ShapeDtypeStruct((n, dm), F32),
        compiler_params=pltpu.CompilerParams(
            dimension_semantics=("parallel",), vmem_limit_bytes=VMEM_LIMIT),
        name="ffn",
    )(x2, mod_l, norm_g, w_in, w_out, final_g)


def _mixin_kernel(x_ref, mod_ref, g_ref, wr_ref, wqk_ref, wm_ref, wg_ref, mu_ref, cw_ref, cb_ref,
                  xs_ref, qk_ref, pm_ref, pg_ref, prev_ref, xbuf, *, tiles_per_seq):
    tm = x_ref.shape[0]
    KC = cw_ref.shape[0]
    first = (pl.program_id(0) % tiles_per_seq) == 0

    @pl.when(first)
    def _():
        prev_ref[...] = jnp.zeros_like(prev_ref)
        xbuf[0:SUBLANES, :] = jnp.zeros((SUBLANES, xbuf.shape[1]), F32)

    @pl.when(jnp.logical_not(first))
    def _():
        xbuf[0:SUBLANES, :] = xbuf[tm:tm + SUBLANES, :]

    h = _rms_mod(x_ref[...], g_ref[...], mod_ref[0, 3:4, :], mod_ref[0, 4:5, :]).astype(BF16)

    pr = jnp.dot(h, wr_ref[...], preferred_element_type=F32)
    row = lax.broadcasted_iota(jnp.int32, (tm, 1), 0)
    prev = jnp.where(row == 0, prev_ref[...], pltpu.roll(pr, 1, axis=0))
    prev_ref[...] = pr[tm - 1:tm, :]
    xs_ref[...] = pr + (prev - pr) * mu_ref[...]

    xbuf[SUBLANES:SUBLANES + tm, :] = jnp.dot(h, wqk_ref[...], preferred_element_type=F32)
    acc = cb_ref[...] + cw_ref[KC - 1:KC, :] * xbuf[SUBLANES:SUBLANES + tm, :]
    for j in range(KC - 1):
        off = SUBLANES - (KC - 1) + j
        acc = acc + cw_ref[j:j + 1, :] * xbuf[off:off + tm, :]
    qk_ref[...] = acc * jax.nn.sigmoid(acc)

    pm_ref[...] = jnp.dot(h, wm_ref[...], preferred_element_type=F32)
    pg_ref[...] = jnp.dot(h, wg_ref[...], preferred_element_type=F32)


def _mixin_call(x2, mod_l, norm_g, w_r, w_qk, w_m, w_g, mu, conv_w, conv_b, *, seq, tm):
    n, dm = x2.shape
    widths = [w.shape[1] for w in (w_r, w_qk, w_m, w_g)]
    assert seq % tm == 0 and conv_w.shape[0] - 1 <= SUBLANES
    kern = functools.partial(_mixin_kernel, tiles_per_seq=seq // tm)
    return pl.pallas_call(
        kern,
        grid=(n // tm,),
        in_specs=[
            pl.BlockSpec((tm, dm), lambda i: (i, 0)),
            pl.BlockSpec((1, 9, dm), lambda i: ((i * tm) // seq, 0, 0)),
            _resident((1, dm)),
            _resident(w_r.shape), _resident(w_qk.shape), _resident(w_m.shape), _resident(w_g.shape),
            _resident(mu.shape), _resident(conv_w.shape), _resident(conv_b.shape),
        ],
        out_specs=[pl.BlockSpec((tm, w), lambda i: (i, 0)) for w in widths],
        out_shape=[jax.ShapeDtypeStruct((n, w), F32) for w in widths],
        scratch_shapes=[pltpu.VMEM((1, widths[0]), F32), pltpu.VMEM((tm + SUBLANES, widths[1]), F32)],
        compiler_params=pltpu.CompilerParams(
            dimension_semantics=("arbitrary",), vmem_limit_bytes=VMEM_LIMIT),
        name="mix_in",
    )(x2, mod_l, norm_g, w_r, w_qk, w_m, w_g, mu, conv_w, conv_b)


def _rwkv_kernel(xs_ref, w0_ref, wup_ref, a0_ref, aup_ref, gup_ref, kk_ref, ka_ref, rk_ref,
                 lng_ref, lnb_ref, o_ref,
                 s_ref, r_s, ld_s, k_s, v_s, a_s, b_s, y_s, bonus_s, g_s, mx_s, n0_s, pc_s):
    tc = xs_ref.shape[0]
    width = o_ref.shape[1]
    C = RWKV_CHUNK
    HD = RWKV_HEAD_DIM
    HG = RWKV_GROUP_HEADS
    GL = HG * HD
    n_groups = width // GL
    shift = HD.bit_length() - 1
    i = pl.program_id(1)

    @pl.when(i == 0)
    def _():
        s_ref[...] = jnp.zeros_like(s_ref)

    r = xs_ref[:, 0:width]
    k = xs_ref[:, width:2 * width]
    v = xs_ref[:, 2 * width:3 * width]
    low = xs_ref[:, 3 * width:3 * width + RWKV_LOWRANK_PAD]

    rg = lax.broadcasted_iota(jnp.int32, (GL, GL), 0)
    cg = lax.broadcasted_iota(jnp.int32, (GL, GL), 1)
    blockmask = (rg >> shift) == (cg >> shift)
    eye = rg == cg
    ones_bd = jnp.where(blockmask, 1.0, 0.0).astype(BF16)

    def segsum(x):
        return jnp.concatenate(
            [_dot_exact_rhs(x[:, g * GL:(g + 1) * GL], ones_bd, 2) for g in range(n_groups)], axis=1)

    w = w0_ref[...] + _bdot(jnp.tanh(low), wup_ref[...])
    ld_s[...] = (-math.exp(-0.5)) * jax.nn.sigmoid(w)
    asig = jax.nn.sigmoid(a0_ref[...] + _bdot(low, aup_ref[...]))
    g_s[...] = _bdot(jax.nn.sigmoid(low), gup_ref[...])
    kk = k * kk_ref[...]
    kk = kk * lax.rsqrt(jnp.maximum(segsum(kk * kk), L2_EPS * L2_EPS))
    k2 = k * (1.0 + (asig - 1.0) * ka_ref[...])
    bonus_s[...] = segsum(r * k2 * rk_ref[...]) * v
    r_s[...] = r
    k_s[...] = k2
    v_s[...] = v
    a_s[...] = -kk
    b_s[...] = kk * asig

    lane = lax.broadcasted_iota(jnp.int32, (C, GL), 1)
    tok = lax.broadcasted_iota(jnp.int32, (C, GL), 0)
    strict = (lane & (HD - 1)) < tok
    incl = (lane & (HD - 1)) <= tok
    headmask = [(lane >> shift) == h for h in range(HG)]
    tri = (lax.broadcasted_iota(jnp.int32, (C, C), 1) <= lax.broadcasted_iota(jnp.int32, (C, C), 0))
    tri = jnp.where(tri, 1.0, 0.0).astype(BF16)

    def bd(x):
        return jnp.where(blockmask, jnp.concatenate([x] * HG, axis=0), 0.0).astype(BF16)

    levels = C.bit_length() - 1

    def prepare_body(c2, carry):
        inst = []
        for u in range(RWKV_PREP_UNROLL):
            c = c2 * RWKV_PREP_UNROLL + u
            sl = pl.ds(pl.multiple_of(c * C, C), C)
            ld = ld_s[sl, :]
            lp = _exact_lhs_dot(tri, ld, 3)
            lpc = lp[C - 1:C, :]
            e_lp = jnp.exp(lp)
            e_nlp = jnp.exp(-lp)
            e_end = jnp.exp(lpc - lp)
            rt = r_s[sl, :] * e_lp
            at = a_s[sl, :] * jnp.exp(lp - ld)
            bb = b_s[sl, :]
            kc = k_s[sl, :]
            bt = bb * e_nlp
            kt = kc * e_nlp
            bh = bb * e_end
            kh = kc * e_end
            vv = v_s[sl, :]
            pc = jnp.exp(lpc)
            for g in range(n_groups):
                gs = slice(g * GL, (g + 1) * GL)
                inst.append(dict(sl=sl, gs=gs, idx=c * n_groups + g, at=at[:, gs], rt=rt[:, gs], v=vv[:, gs],
                                 bt=bt[:, gs], kt=kt[:, gs], bh=bh[:, gs], kh=kh[:, gs], pc=pc[:, gs]))
        for t in inst:
            lhs = jnp.concatenate([t['at'], t['rt']], axis=0)
            rows = [jnp.where(headmask[h], t['bt'], 0.0) for h in range(HG)]
            rows += [jnp.where(headmask[h], t['kt'], 0.0) for h in range(HG)]
            gram = _bdot_nt(lhs, jnp.concatenate(rows, axis=0))
            t['a_ab'] = jnp.where(strict, gram[:C, :GL], 0.0)
            t['a_ak'] = jnp.where(strict, gram[:C, GL:], 0.0)
            t['a_rb'] = jnp.where(incl, gram[C:, :GL], 0.0).astype(BF16)
            t['a_rk'] = jnp.where(incl, gram[C:, GL:], 0.0)
        for t in inst:
            t['bdv'] = bd(t['v'])
            both = _bdot(jnp.concatenate([t['a_ak'], t['a_rk']], axis=0), t['bdv'])
            t['akv'] = both[:C]
            t['arkv'] = both[C:]
        for t in inst:
            t['tr'] = t['a_ab']
            t['nm'] = jnp.dot(t['a_ab'].astype(BF16), bd(t['a_ab']), preferred_element_type=F32)
        for lvl in range(1, levels):
            for t in inst:
                if lvl + 1 < levels:
                    both = _bdot(jnp.concatenate([t['tr'], t['nm']], axis=0), bd(t['nm']))
                    t['tr'] = t['tr'] + t['nm'] + both[:C]
                    t['nm'] = both[C:]
                else:
                    t['tr'] = t['tr'] + t['nm'] + _bdot(t['tr'], bd(t['nm']))
        for t in inst:
            wu = _bdot(t['tr'], jnp.concatenate([bd(t['at']), bd(t['akv'])], axis=1))
            t['w'] = t['at'] + wu[:, :GL]
            t['u0'] = t['akv'] + wu[:, GL:]
        for t in inst:
            ry = jnp.dot(t['a_rb'], jnp.concatenate([bd(t['w']), bd(t['u0'])], axis=1),
                         preferred_element_type=F32)
            r_s[t['sl'], t['gs']] = t['rt'] + ry[:, :GL]
            y_s[t['sl'], t['gs']] = ry[:, GL:] + t['arkv']
        for t in inst:
            mn = _bdot_tn(t['bh'], jnp.concatenate([t['w'], t['u0']], axis=1))
            mx_s[t['idx']] = jnp.where(blockmask, mn[:, :GL], 0.0).astype(BF16)
            n0_s[t['idx']] = jnp.where(blockmask, mn[:, GL:] + _bdot_tn(t['kh'], t['v']), 0.0)
            pc_s[t['idx']] = jnp.broadcast_to(
                jnp.sum(jnp.where(eye, jnp.broadcast_to(t['pc'], (GL, GL)), 0.0), axis=1, keepdims=True),
                (GL, LANES))
        return carry

    lax.fori_loop(0, tc // (C * RWKV_PREP_UNROLL), prepare_body, 0)

    def advance(c, carry):
        sl = pl.ds(pl.multiple_of(c * C, C), C)
        for g in range(n_groups):
            gs = slice(g * GL, (g + 1) * GL)
            idx = c * n_groups + g
            s_old = s_ref[g]
            s_bf = s_old.astype(BF16)
            y_s[sl, gs] = y_s[sl, gs] + jnp.dot(r_s[sl, gs].astype(BF16), s_bf, preferred_element_type=F32)
            s_ref[g] = (pc_s[idx][:, 0:1] * s_old + n0_s[idx]
                        + jnp.dot(mx_s[idx], s_bf, preferred_element_type=F32))
        return carry

    lax.fori_loop(0, tc // C, advance, 0)

    y = y_s[...]
    inv = 1.0 / HD
    yc = y - segsum(y) * inv
    var = segsum(yc * yc) * inv
    yn = yc * lax.rsqrt(var + RWKV_GN_EPS) * lng_ref[...] + lnb_ref[...]
    o_ref[...] = (yn + bonus_s[...]) * g_s[...]


def _rwkv_call(xs, w0, wup, a0, aup, gup, k_k, k_a, r_k, ln_g, ln_b, *, batch, seq, tc):
    n, ncols = xs.shape
    width = RWKV_HEADS * RWKV_HEAD_DIM
    gl = RWKV_GROUP_HEADS * RWKV_HEAD_DIM
    nt = seq // tc
    assert seq % tc == 0 and tc % (RWKV_CHUNK * RWKV_PREP_UNROLL) == 0 and width % gl == 0
    nblk = (tc // RWKV_CHUNK) * (width // gl)
    vec = _resident((1, width))
    lowmat = _resident((RWKV_LOWRANK_PAD, width))
    tile = pltpu.VMEM((tc, width), F32)
    return pl.pallas_call(
        _rwkv_kernel,
        grid=(batch, nt),
        in_specs=[
            pl.BlockSpec((tc, ncols), lambda b, i: (b * nt + i, 0)),
            vec, lowmat, vec, lowmat, lowmat, vec, vec, vec, vec, vec,
        ],
        out_specs=pl.BlockSpec((tc, width), lambda b, i: (b * nt + i, 0)),
        out_shape=jax.ShapeDtypeStruct((n, width), F32),
        scratch_shapes=[
            pltpu.VMEM((width // gl, gl, gl), F32),
            tile, tile, tile, tile, tile, tile, tile, tile, tile,
            pltpu.VMEM((nblk, gl, gl), BF16),
            pltpu.VMEM((nblk, gl, gl), F32),
            pltpu.VMEM((nblk, gl, LANES), F32),
        ],
        compiler_params=pltpu.CompilerParams(
            dimension_semantics=("parallel", "arbitrary"), vmem_limit_bytes=VMEM_LIMIT),
        name="rwkv7",
    )(xs, w0, wup, a0, aup, gup, k_k, k_a, r_k, ln_g, ln_b)


def _cummax_rows(x):
    n = x.shape[0]
    row = lax.broadcasted_iota(jnp.int32, x.shape, 0)
    s = 1
    while s < n:
        x = jnp.maximum(x, jnp.where(row >= s, pltpu.roll(x, s, axis=0), -jnp.inf))
        s *= 2
    return x


def _mlstm_kernel(qk_s, p_ref, gbi_ref, gbf_ref, lng_ref, o_ref, cn_ref, m_ref):
    tcm = p_ref.shape[0]
    NH, DH, L = MLSTM_HEADS, MLSTM_HEAD_DIM, MLSTM_CHUNK
    wd = NH * DH
    i = pl.program_id(1)

    @pl.when(i == 0)
    def _():
        cn_ref[...] = jnp.zeros_like(cn_ref)
        m_ref[...] = jnp.full(m_ref.shape, -jnp.inf, F32)

    rr = lax.broadcasted_iota(jnp.int32, (L, L), 0)
    cc = lax.broadcasted_iota(jnp.int32, (L, L), 1)
    causal = cc <= rr
    tri = jnp.where(causal, 1.0, 0.0).astype(BF16)
    eye_l = jnp.where(cc == rr, 1.0, 0.0).astype(BF16)
    eye_h = eye_l[0:2 * SUBLANES, :]
    ones_v = jnp.ones((L, DH), BF16)
    scale = DH ** -0.5
    heads = range(NH)

    def chunk(c, carry):
        sl = pl.ds(pl.multiple_of(c * L, L), L)
        li = p_ref[sl, 2 * wd:2 * wd + LANES] + gbi_ref[...]
        lf = -_softplus(-(p_ref[sl, 2 * wd + LANES:2 * wd + 2 * LANES] + gbf_ref[...]))
        b_col = _exact_lhs_dot(tri, lf, 3)
        xcol = li - b_col
        m_prev = m_ref[...]
        zcol = jnp.maximum(m_prev, _cummax_rows(xcol))
        b_end = b_col[L - 1:L, :]
        m_new = jnp.maximum(b_end + m_prev, jnp.max(b_end + xcol, axis=0, keepdims=True))
        keep = jnp.exp(b_end + m_prev - m_new)
        m_ref[...] = m_new
        wshift = b_end - m_new
        xrow = None
        for part in _split_bf16(xcol, 3):
            t = lax.dot_general(eye_h, part, (((1,), (1,)), ((), ())), preferred_element_type=F32)
            xrow = t if xrow is None else xrow + t

        q = [(qk_s[sl, h * DH:(h + 1) * DH] * scale).astype(BF16) for h in heads]
        k = [qk_s[sl, wd + h * DH:wd + (h + 1) * DH].astype(BF16) for h in heads]
        vx = [jnp.concatenate([p_ref[sl, h * DH:(h + 1) * DH].astype(BF16), ones_v], axis=1)
              for h in heads]
        cn = [cn_ref[h] for h in heads]
        s = [lax.dot_general(q[h], k[h], (((1,), (1,)), ((), ())), preferred_element_type=F32) for h in heads]
        kt = [lax.dot_general(eye_l, k[h], (((1,), (1,)), ((), ())), preferred_element_type=F32) for h in heads]
        qc = [jnp.dot(q[h], cn[h].astype(BF16), preferred_element_type=F32) for h in heads]
        zb = [jnp.broadcast_to(zcol[:, h:h + 1], (L, DH)) for h in heads]
        bb = [jnp.broadcast_to(b_col[:, h:h + 1], (L, DH)) for h in heads]
        pw = [(s[h] * jnp.exp(jnp.where(causal, xrow[h:h + 1, :] - zb[h], -jnp.inf))).astype(BF16) for h in heads]
        nd = [jnp.dot(pw[h], vx[h], preferred_element_type=F32) for h in heads]
        for h in heads:
            wrow = jnp.exp(xrow[h:h + 1, :] + wshift[:, h:h + 1])
            cn_ref[h] = keep[:, h:h + 1] * cn[h] + jnp.dot((kt[h] * wrow).astype(BF16), vx[h],
                                                            preferred_element_type=F32)
        for h in heads:
            hs = slice(h * DH, (h + 1) * DH)
            inter = jnp.exp(m_prev[:, h:h + 1] - zb[h])
            num = nd[h][:, :DH] + inter * qc[h][:, :DH]
            den = nd[h][:, DH:] + inter * qc[h][:, DH:]
            hh = num / jnp.maximum(jnp.abs(den), jnp.exp(-(bb[h] + zb[h])))
            hc = hh - jnp.mean(hh, axis=1, keepdims=True)
            hn = hc * lax.rsqrt(jnp.mean(hc * hc, axis=1, keepdims=True) + MLSTM_NORM_EPS)
            og = jax.nn.sigmoid(p_ref[sl, wd + h * DH:wd + (h + 1) * DH])
            o_ref[sl, hs] = og * (hn * lng_ref[:, hs])
        return carry

    lax.fori_loop(0, tcm // L, chunk, 0)


def _mlstm_call(qk, p_m, gate_bi, gate_bf, ln_g, *, batch, seq, tcm):
    n, ncols = p_m.shape
    nh, dh = MLSTM_HEADS, MLSTM_HEAD_DIM
    wd = nh * dh
    nt = seq // tcm
    assert seq % tcm == 0 and tcm % MLSTM_CHUNK == 0
    return pl.pallas_call(
        _mlstm_kernel,
        grid=(batch, nt),
        in_specs=[
            pl.BlockSpec((tcm, 2 * wd), lambda b, i: (b * nt + i, 0)),
            pl.BlockSpec((tcm, ncols), lambda b, i: (b * nt + i, 0)),
            _resident((1, LANES)),
            _resident((1, LANES)),
            _resident((1, wd)),
        ],
        out_specs=pl.BlockSpec((tcm, wd), lambda b, i: (b * nt + i, 0)),
        out_shape=jax.ShapeDtypeStruct((n, wd), F32),
        scratch_shapes=[
            pltpu.VMEM((nh, dh, 2 * dh), F32),
            pltpu.VMEM((1, LANES), F32),
        ],
        compiler_params=pltpu.CompilerParams(
            dimension_semantics=("parallel", "arbitrary"), vmem_limit_bytes=VMEM_LIMIT),
        name="mlstm",
    )(qk, p_m, gate_bi, gate_bf, ln_g)


def _mixout_ffn_kernel(x_ref, yr_ref, ym_ref, pg_ref, mod_ref, wr_ref, wm_ref, wo_ref,
                       g_ref, wi_ref, wf_ref, fg_ref, o_ref, *, final_norm):
    dm = x_ref.shape[1]
    zr = jnp.dot(yr_ref[...].astype(BF16), wr_ref[...], preferred_element_type=F32)
    zm = jnp.dot(ym_ref[...].astype(BF16), wm_ref[...], preferred_element_type=F32)
    mix = jax.nn.sigmoid(pg_ref[:, 0:dm]) * zr + jax.nn.sigmoid(pg_ref[:, dm:2 * dm]) * zm
    out = jnp.dot(mix.astype(BF16), wo_ref[...], preferred_element_type=F32)
    x = x_ref[...] + (1.0 + mod_ref[0, 5:6, :]) * out
    o_ref[...] = _ffn_half_step(x, mod_ref, g_ref, wi_ref, wf_ref, fg_ref, mod_row=6, final_norm=final_norm)


def _mixout_ffn_call(x2, y_r, y_m, p_g, mod_l, w_r, w_m, w_o, norm_g, w_in, w_out, final_g, *,
                     seq, final_norm, tm):
    n, dm = x2.shape
    rowblk = lambda cols: pl.BlockSpec((tm, cols), lambda i: (i, 0))
    kern = functools.partial(_mixout_ffn_kernel, final_norm=final_norm)
    return pl.pallas_call(
        kern,
        grid=(n // tm,),
        in_specs=[
            rowblk(dm), rowblk(y_r.shape[1]), rowblk(y_m.shape[1]), rowblk(p_g.shape[1]),
            pl.BlockSpec((1, 9, dm), lambda i: ((i * tm) // seq, 0, 0)),
            _resident(w_r.shape), _resident(w_m.shape), _resident(w_o.shape),
            _resident((1, dm)), _resident(w_in.shape), _resident(w_out.shape), _resident((1, dm)),
        ],
        out_specs=rowblk(dm),
        out_shape=jax.ShapeDtypeStruct((n, dm), F32),
        compiler_params=pltpu.CompilerParams(
            dimension_semantics=("parallel",), vmem_limit_bytes=VMEM_LIMIT),
        name="mix_out_ffn",
    )(x2, y_r, y_m, p_g, mod_l, w_r, w_m, w_o, norm_g, w_in, w_out, final_g)


def _pad_cols(a, n):
    return jnp.pad(a, ((0, 0), (0, n - a.shape[1])))


def _row(a, n=None):
    a = a.reshape(1, -1)
    return a if n is None else _pad_cols(a, n)


def _lowrank_pad(mat, row0):
    out = jnp.zeros((RWKV_LOWRANK_PAD, mat.shape[1]), F32)
    return lax.dynamic_update_slice(out, mat, (row0, 0)).astype(BF16)


def kernel(x, c, ada_w, ada_b, norm_g, ffn_w_in, ffn_w_out, mix_w_in, rwkv_mu, rwkv_w0, rwkv_w_up, rwkv_a0, rwkv_a_up, rwkv_g_up, rwkv_k_k, rwkv_k_a, rwkv_r_k, rwkv_ln_g, rwkv_ln_b, mlstm_conv_w, mlstm_conv_b, mlstm_gate_b, mlstm_ln_g, branch_w_rwkv, branch_w_mlstm, mix_w_out, final_g):
    batch, seq, dm = x.shape
    depth = ada_w.shape[0]
    n = batch * seq
    rw = RWKV_HEADS * RWKV_HEAD_DIM
    mw = MLSTM_HEADS * MLSTM_HEAD_DIM
    rwkv_cols = 3 * rw + RWKV_DECAY_RANK + RWKV_A_RANK + RWKV_GATE_RANK
    mlstm_cols = 4 * mw + 2 * MLSTM_HEADS
    rwkv_pad = 3 * rw + RWKV_LOWRANK_PAD

    tm_ffn = min(512, seq)
    tm_mix = min(512, seq)
    tc_rwkv = min(512, seq)
    tc_mlstm = min(512, seq)

    c_pad = jnp.pad(c, ((0, SUBLANES - batch % SUBLANES if batch % SUBLANES else 0), (0, 0)))
    mod = _ada_call(c_pad, ada_w, ada_b)[:, :batch].reshape(depth, batch, 9, dm)

    ffn_w_in_bf = ffn_w_in.astype(BF16)
    ffn_w_out_bf = ffn_w_out.astype(BF16)
    final_row = _row(final_g)

    x2 = x.reshape(n, dm)
    for l in range(depth):
        mod_l = mod[l]
        x2 = _ffn_call(x2, mod_l, _row(norm_g[l, 0]), ffn_w_in_bf[l, 0], ffn_w_out_bf[l, 0], final_row,
                       seq=seq, mod_row=0, final_norm=False, tm=tm_ffn)

        w_in = mix_w_in[l]
        w_r = _pad_cols(w_in[:, :rwkv_cols], rwkv_pad).astype(BF16)
        v0 = rwkv_cols + 2 * mw
        gate0 = rwkv_cols + 4 * mw
        w_qk = w_in[:, rwkv_cols:v0].astype(BF16)
        w_m = jnp.concatenate([
            w_in[:, v0:gate0],
            _pad_cols(w_in[:, gate0:gate0 + MLSTM_HEADS], LANES),
            _pad_cols(w_in[:, gate0 + MLSTM_HEADS:gate0 + 2 * MLSTM_HEADS], LANES)], axis=1).astype(BF16)
        w_g = w_in[:, rwkv_cols + mlstm_cols:].astype(BF16)
        xs_r, qk_m, p_m, p_g = _mixin_call(
            x2, mod_l, _row(norm_g[l, 1]), w_r, w_qk, w_m, w_g, _row(rwkv_mu[l], rwkv_pad),
            mlstm_conv_w[l], _row(mlstm_conv_b[l]), seq=seq, tm=tm_mix)

        y_r = _rwkv_call(
            xs_r, _row(rwkv_w0[l]),
            _lowrank_pad(rwkv_w_up[l], 0), _row(rwkv_a0[l]),
            _lowrank_pad(rwkv_a_up[l], RWKV_DECAY_RANK),
            _lowrank_pad(rwkv_g_up[l], RWKV_DECAY_RANK + RWKV_A_RANK),
            _row(rwkv_k_k[l]), _row(rwkv_k_a[l]), _row(rwkv_r_k[l]), _row(rwkv_ln_g[l]), _row(rwkv_ln_b[l]),
            batch=batch, seq=seq, tc=tc_rwkv)
        y_m = _mlstm_call(qk_m, p_m,
                          _row(mlstm_gate_b[l, :MLSTM_HEADS], LANES), _row(mlstm_gate_b[l, MLSTM_HEADS:], LANES),
                          _row(mlstm_ln_g[l]), batch=batch, seq=seq, tcm=tc_mlstm)

        x2 = _mixout_ffn_call(
            x2, y_r, y_m, p_g, mod_l, branch_w_rwkv[l].astype(BF16), branch_w_mlstm[l].astype(BF16),
            mix_w_out[l].astype(BF16), _row(norm_g[l, 2]), ffn_w_in_bf[l, 1], ffn_w_out_bf[l, 1], final_row,
            seq=seq, final_norm=(l == depth - 1), tm=tm_ffn)
    return x2.reshape(batch, seq, dm)
```

```python
import functools
import math

import jax
import jax.numpy as jnp
from jax import lax
from jax.experimental import pallas as pl
from jax.experimental.pallas import tpu as pltpu

F32 = jnp.float32
BF16 = jnp.bfloat16

RWKV_HEADS = 8
RWKV_HEAD_DIM = 64
RWKV_DECAY_RANK = 32
RWKV_A_RANK = 32
RWKV_GATE_RANK = 96
MLSTM_HEADS = 4
MLSTM_HEAD_DIM = 128
MLSTM_CONV = 4
MLSTM_CHUNK = 128
RMS_EPS = 1e-6
RWKV_GN_EPS = 64e-5
MLSTM_NORM_EPS = 1e-5
L2_EPS = 1e-12

LANES = 128
SUBLANES = 8
VMEM_LIMIT = 56 * 1024 * 1024
ROW_SPLIT = 2

RWKV_CHUNK = 64
RWKV_GROUP_HEADS = 2
RWKV_PREP_UNROLL = 4
RWKV_LOWRANK_PAD = 256


def _bdot(a, b):
    return jnp.dot(a.astype(BF16), b.astype(BF16), preferred_element_type=F32)


def _bdot_nt(a, b):
    return lax.dot_general(a.astype(BF16), b.astype(BF16), (((1,), (1,)), ((), ())),
                           preferred_element_type=F32)


def _bdot_tn(a, b):
    return lax.dot_general(a.astype(BF16), b.astype(BF16), (((0,), (0,)), ((), ())),
                           preferred_element_type=F32)


def _split_bf16(x, parts):
    out = []
    rem = x
    for _ in range(parts):
        p = rem.astype(BF16)
        out.append(p)
        rem = rem - p.astype(F32)
    return out


def _dot_exact_rhs(x, m_bf16, parts):
    acc = None
    for p in _split_bf16(x, parts):
        t = jnp.dot(p, m_bf16, preferred_element_type=F32)
        acc = t if acc is None else acc + t
    return acc


def _exact_lhs_dot(m_bf16, x, parts):
    acc = None
    for p in _split_bf16(x, parts):
        t = jnp.dot(m_bf16, p, preferred_element_type=F32)
        acc = t if acc is None else acc + t
    return acc


def _rms_mod(x, g, shift, scale):
    y = x * lax.rsqrt(jnp.mean(x * x, axis=-1, keepdims=True) + RMS_EPS) * g
    return y * (1.0 + scale) + shift


def _softplus(z):
    return jnp.maximum(z, 0.0) + jnp.log1p(jnp.exp(-jnp.abs(z)))


def _ada_kernel(c_ref, w_ref, b_ref, o_ref):
    c = c_ref[...]
    cond = c * jax.nn.sigmoid(c)
    o_ref[0] = _bdot(cond, w_ref[0]) + b_ref[0]


def _ada_call(c_pad, ada_w, ada_b):
    depth, dm, n9 = ada_w.shape
    tn = n9 // 4
    rows = c_pad.shape[0]
    return pl.pallas_call(
        _ada_kernel,
        grid=(depth, n9 // tn),
        in_specs=[
            pl.BlockSpec((rows, dm), lambda l, j: (0, 0)),
            pl.BlockSpec((1, dm, tn), lambda l, j: (l, 0, j)),
            pl.BlockSpec((1, 1, tn), lambda l, j: (l, 0, j)),
        ],
        out_specs=pl.BlockSpec((1, rows, tn), lambda l, j: (l, 0, j)),
        out_shape=jax.ShapeDtypeStruct((depth, rows, n9), F32),
        compiler_params=pltpu.CompilerParams(
            dimension_semantics=("arbitrary", "arbitrary"), vmem_limit_bytes=VMEM_LIMIT),
        name="ada_mod",
    )(c_pad, ada_w, ada_b.reshape(depth, 1, n9))


def _row_parts(ref, parts):
    rows = ref.shape[0] // parts
    return [ref[k * rows:(k + 1) * rows, :] for k in range(parts)]


def _store_row_parts(ref, vals):
    rows = ref.shape[0] // len(vals)
    for k, v in enumerate(vals):
        ref[k * rows:(k + 1) * rows, :] = v


def _ffn_half_step(xs, mod_ref, g_ref, wi_ref, wo_ref, fg_ref, *, mod_row, final_norm):
    dff = wo_ref.shape[0]
    shift, scale = mod_ref[0, mod_row:mod_row + 1, :], mod_ref[0, mod_row + 1:mod_row + 2, :]
    hs = [_rms_mod(x, g_ref[...], shift, scale).astype(BF16) for x in xs]
    gus = [jnp.dot(h, wi_ref[...], preferred_element_type=F32) for h in hs]
    acts = [((gu[:, :dff] * jax.nn.sigmoid(gu[:, :dff])) * gu[:, dff:]).astype(BF16) for gu in gus]
    ys = [jnp.dot(a, wo_ref[...], preferred_element_type=F32) for a in acts]
    outs = [x + (0.5 * (1.0 + mod_ref[0, mod_row + 2:mod_row + 3, :])) * y for x, y in zip(xs, ys)]
    if final_norm:
        outs = [o * lax.rsqrt(jnp.mean(o * o, axis=-1, keepdims=True) + RMS_EPS) * fg_ref[...] for o in outs]
    return outs


def _ffn_kernel(x_ref, mod_ref, g_ref, wi_ref, wo_ref, fg_ref, o_ref, *, mod_row, final_norm):
    outs = _ffn_half_step(_row_parts(x_ref, ROW_SPLIT), mod_ref, g_ref, wi_ref, wo_ref, fg_ref,
                          mod_row=mod_row, final_norm=final_norm)
    _store_row_parts(o_ref, outs)


def _resident(shape):
    return pl.BlockSpec(shape, lambda *_: (0,) * len(shape), pipeline_mode=pl.Buffered(1))


def _ffn_call(x2, mod_l, norm_g, w_in, w_out, final_g, *, seq, mod_row, final_norm, tm):
    n, dm = x2.shape
    kern = functools.partial(_ffn_kernel, mod_row=mod_row, final_norm=final_norm)
    return pl.pallas_call(
        kern,
        grid=(n // tm,),
        in_specs=[
            pl.BlockSpec((tm, dm), lambda i: (i, 0)),
            pl.BlockSpec((1, 9, dm), lambda i: ((i * tm) // seq, 0, 0)),
            _resident((1, dm)),
            _resident(w_in.shape),
            _resident(w_out.shape),
            _resident((1, dm)),
        ],
        out_specs=pl.BlockSpec((tm, dm), lambda i: (i, 0)),
        out_shape=jax.ShapeDtypeStruct((n, dm), F32),
        compiler_params=pltpu.CompilerParams(
            dimension_semantics=("parallel",), vmem_limit_bytes=VMEM_LIMIT),
        name="ffn",
    )(x2, mod_l, norm_g, w_in, w_out, final_g)


def _mixin_kernel(x_ref, mod_ref, g_ref, w_ref, mu_ref, cw_ref, cb_ref,
                  xs_ref, qk_ref, pm_ref, pg_ref, prev_ref, xbuf, *, tiles_per_seq):
    tm = x_ref.shape[0]
    KC = cw_ref.shape[0]
    first = (pl.program_id(0) % tiles_per_seq) == 0
    c1 = qk_ref.shape[1]
    c2 = c1 + xs_ref.shape[1]
    c3 = c2 + pm_ref.shape[1]
    c4 = c3 + pg_ref.shape[1]

    @pl.when(first)
    def _():
        prev_ref[...] = jnp.zeros_like(prev_ref)
        xbuf[0:SUBLANES, :] = jnp.zeros((SUBLANES, xbuf.shape[1]), F32)

    @pl.when(jnp.logical_not(first))
    def _():
        xbuf[0:SUBLANES, :] = xbuf[tm:tm + SUBLANES, :]

    h = _rms_mod(x_ref[...], g_ref[...], mod_ref[0, 3:4, :], mod_ref[0, 4:5, :]).astype(BF16)

    pr = jnp.dot(h, w_ref[:, c1:c2], preferred_element_type=F32)
    row = lax.broadcasted_iota(jnp.int32, (tm, 1), 0)
    prev = jnp.where(row == 0, prev_ref[...], pltpu.roll(pr, 1, axis=0))
    prev_ref[...] = pr[tm - 1:tm, :]
    xs_ref[...] = pr + (prev - pr) * mu_ref[...]

    xbuf[SUBLANES:SUBLANES + tm, :] = jnp.dot(h, w_ref[:, 0:c1], preferred_element_type=F32)
    acc = cb_ref[...] + cw_ref[KC - 1:KC, :] * xbuf[SUBLANES:SUBLANES + tm, :]
    for j in range(KC - 1):
        off = SUBLANES - (KC - 1) + j
        acc = acc + cw_ref[j:j + 1, :] * xbuf[off:off + tm, :]
    qk_ref[...] = acc * jax.nn.sigmoid(acc)

    pm_ref[...] = jnp.dot(h, w_ref[:, c2:c3], preferred_element_type=F32)
    pg_ref[...] = jnp.dot(h, w_ref[:, c3:c4], preferred_element_type=F32)


def _mixin_call(x2, mod_l, norm_g, w_all, widths, mu, conv_w, conv_b, *, seq, tm):
    n, dm = x2.shape
    assert seq % tm == 0 and conv_w.shape[0] - 1 <= SUBLANES and sum(widths) == w_all.shape[1]
    kern = functools.partial(_mixin_kernel, tiles_per_seq=seq // tm)
    return pl.pallas_call(
        kern,
        grid=(n // tm,),
        in_specs=[
            pl.BlockSpec((tm, dm), lambda i: (i, 0)),
            pl.BlockSpec((1, 9, dm), lambda i: ((i * tm) // seq, 0, 0)),
            _resident((1, dm)),
            _resident(w_all.shape),
            _resident(mu.shape), _resident(conv_w.shape), _resident(conv_b.shape),
        ],
        out_specs=[pl.BlockSpec((tm, w), lambda i: (i, 0)) for w in widths],
        out_shape=[jax.ShapeDtypeStruct((n, w), F32) for w in widths],
        scratch_shapes=[pltpu.VMEM((1, widths[0]), F32), pltpu.VMEM((tm + SUBLANES, widths[1]), F32)],
        compiler_params=pltpu.CompilerParams(
            dimension_semantics=("arbitrary",), vmem_limit_bytes=VMEM_LIMIT),
        name="mix_in",
    )(x2, mod_l, norm_g, w_all, mu, conv_w, conv_b)


def _rwkv_kernel(xs_ref, w0_ref, wup_ref, a0_ref, aup_ref, gup_ref, kk_ref, ka_ref, rk_ref,
                 lng_ref, lnb_ref, o_ref,
                 s_ref, r_s, ld_s, k_s, v_s, a_s, b_s, y_s, bonus_s, g_s, mx_s, n0_s, pc_s):
    tc = xs_ref.shape[0]
    width = o_ref.shape[1]
    C = RWKV_CHUNK
    HD = RWKV_HEAD_DIM
    HG = RWKV_GROUP_HEADS
    GL = HG * HD
    n_groups = width // GL
    shift = HD.bit_length() - 1
    i = pl.program_id(1)

    @pl.when(i == 0)
    def _():
        s_ref[...] = jnp.zeros_like(s_ref)

    r = xs_ref[:, 0:width]
    k = xs_ref[:, width:2 * width]
    v = xs_ref[:, 2 * width:3 * width]
    low = xs_ref[:, 3 * width:3 * width + RWKV_LOWRANK_PAD]

    rg = lax.broadcasted_iota(jnp.int32, (GL, GL), 0)
    cg = lax.broadcasted_iota(jnp.int32, (GL, GL), 1)
    blockmask = (rg >> shift) == (cg >> shift)
    eye = rg == cg
    ones_bd = jnp.where(blockmask, 1.0, 0.0).astype(BF16)

    def segsum(x):
        return jnp.concatenate(
            [_dot_exact_rhs(x[:, g * GL:(g + 1) * GL], ones_bd, 2) for g in range(n_groups)], axis=1)

    w = w0_ref[...] + _bdot(jnp.tanh(low), wup_ref[...])
    ld_s[...] = (-math.exp(-0.5)) * jax.nn.sigmoid(w)
    asig = jax.nn.sigmoid(a0_ref[...] + _bdot(low, aup_ref[...]))
    g_s[...] = _bdot(jax.nn.sigmoid(low), gup_ref[...])
    kk = k * kk_ref[...]
    kk = kk * lax.rsqrt(jnp.maximum(segsum(kk * kk), L2_EPS * L2_EPS))
    k2 = k * (1.0 + (asig - 1.0) * ka_ref[...])
    bonus_s[...] = segsum(r * k2 * rk_ref[...]) * v
    r_s[...] = r
    k_s[...] = k2
    v_s[...] = v
    a_s[...] = -kk
    b_s[...] = kk * asig

    lane = lax.broadcasted_iota(jnp.int32, (C, GL), 1)
    tok = lax.broadcasted_iota(jnp.int32, (C, GL), 0)
    strict = (lane & (HD - 1)) < tok
    incl = (lane & (HD - 1)) <= tok
    headmask = [(lane >> shift) == h for h in range(HG)]
    tri = (lax.broadcasted_iota(jnp.int32, (C, C), 1) <= lax.broadcasted_iota(jnp.int32, (C, C), 0))
    tri = jnp.where(tri, 1.0, 0.0).astype(BF16)

    def bd(x):
        return jnp.where(blockmask, jnp.concatenate([x] * HG, axis=0), 0.0).astype(BF16)

    levels = C.bit_length() - 1

    def prepare_stages(c_first):
        inst = []
        for u in range(RWKV_PREP_UNROLL):
            c = c_first + u
            sl = slice(c * C, (c + 1) * C)
            ld = ld_s[sl, :]
            lp = _exact_lhs_dot(tri, ld, 3)
            lpc = lp[C - 1:C, :]
            e_lp = jnp.exp(lp)
            e_nlp = jnp.exp(-lp)
            e_end = jnp.exp(lpc - lp)
            rt = r_s[sl, :] * e_lp
            at = a_s[sl, :] * jnp.exp(lp - ld)
            bb = b_s[sl, :]
            kc = k_s[sl, :]
            bt = bb * e_nlp
            kt = kc * e_nlp
            bh = bb * e_end
            kh = kc * e_end
            vv = v_s[sl, :]
            pc = jnp.exp(lpc)
            for g in range(n_groups):
                gs = slice(g * GL, (g + 1) * GL)
                inst.append(dict(sl=sl, gs=gs, idx=c * n_groups + g, at=at[:, gs], rt=rt[:, gs], v=vv[:, gs],
                                 bt=bt[:, gs], kt=kt[:, gs], bh=bh[:, gs], kh=kh[:, gs], pc=pc[:, gs]))
        for t in inst:
            lhs = jnp.concatenate([t['at'], t['rt']], axis=0)
            rows = [jnp.where(headmask[h], t['bt'], 0.0) for h in range(HG)]
            rows += [jnp.where(headmask[h], t['kt'], 0.0) for h in range(HG)]
            gram = _bdot_nt(lhs, jnp.concatenate(rows, axis=0))
            t['a_ab'] = jnp.where(strict, gram[:C, :GL], 0.0)
            t['a_ak'] = jnp.where(strict, gram[:C, GL:], 0.0)
            t['a_rb'] = jnp.where(incl, gram[C:, :GL], 0.0).astype(BF16)
            t['a_rk'] = jnp.where(incl, gram[C:, GL:], 0.0)
        yield
        for t in inst:
            t['bdv'] = bd(t['v'])
            both = _bdot(jnp.concatenate([t['a_ak'], t['a_rk']], axis=0), t['bdv'])
            t['akv'] = both[:C]
            t['arkv'] = both[C:]
        yield
        for t in inst:
            t['tr'] = t['a_ab']
            t['nm'] = jnp.dot(t['a_ab'].astype(BF16), bd(t['a_ab']), preferred_element_type=F32)
        yield
        for lvl in range(1, levels):
            for t in inst:
                if lvl + 1 < levels:
                    both = _bdot(jnp.concatenate([t['tr'], t['nm']], axis=0), bd(t['nm']))
                    t['tr'] = t['tr'] + t['nm'] + both[:C]
                    t['nm'] = both[C:]
                else:
                    t['tr'] = t['tr'] + t['nm'] + _bdot(t['tr'], bd(t['nm']))
            yield
        for t in inst:
            wu = _bdot(t['tr'], jnp.concatenate([bd(t['at']), bd(t['akv'])], axis=1))
            t['w'] = t['at'] + wu[:, :GL]
            t['u0'] = t['akv'] + wu[:, GL:]
        yield
        for t in inst:
            ry = jnp.dot(t['a_rb'], jnp.concatenate([bd(t['w']), bd(t['u0'])], axis=1),
                         preferred_element_type=F32)
            r_s[t['sl'], t['gs']] = t['rt'] + ry[:, :GL]
            y_s[t['sl'], t['gs']] = ry[:, GL:] + t['arkv']
        yield
        for t in inst:
            mn = _bdot_tn(t['bh'], jnp.concatenate([t['w'], t['u0']], axis=1))
            mx_s[t['idx']] = jnp.where(blockmask, mn[:, :GL], 0.0).astype(BF16)
            n0_s[t['idx']] = jnp.where(blockmask, mn[:, GL:] + _bdot_tn(t['kh'], t['v']), 0.0)
            pc_s[t['idx']] = jnp.broadcast_to(
                jnp.sum(jnp.where(eye, jnp.broadcast_to(t['pc'], (GL, GL)), 0.0), axis=1, keepdims=True),
                (GL, GL))

    def advance(c):
        sl = slice(c * C, (c + 1) * C)
        for g in range(n_groups):
            gs = slice(g * GL, (g + 1) * GL)
            idx = c * n_groups + g
            s_old = s_ref[g]
            s_bf = s_old.astype(BF16)
            y_s[sl, gs] = y_s[sl, gs] + jnp.dot(r_s[sl, gs].astype(BF16), s_bf, preferred_element_type=F32)
            s_ref[g] = pc_s[idx] * s_old + n0_s[idx] + jnp.dot(mx_s[idx], s_bf, preferred_element_type=F32)

    n_chunks = tc // C
    pending = []
    for c_first in range(0, n_chunks, RWKV_PREP_UNROLL):
        stages = prepare_stages(c_first)
        for k, _ in enumerate(stages):
            if pending and k % 2 == 1:
                advance(pending.pop(0))
        for c in pending:
            advance(c)
        pending = list(range(c_first, c_first + RWKV_PREP_UNROLL))
    for c in pending:
        advance(c)

    y = y_s[...]
    inv = 1.0 / HD
    yc = y - segsum(y) * inv
    var = segsum(yc * yc) * inv
    yn = yc * lax.rsqrt(var + RWKV_GN_EPS) * lng_ref[...] + lnb_ref[...]
    o_ref[...] = (yn + bonus_s[...]) * g_s[...]


def _rwkv_call(xs, w0, wup, a0, aup, gup, k_k, k_a, r_k, ln_g, ln_b, *, batch, seq, tc):
    n, ncols = xs.shape
    width = RWKV_HEADS * RWKV_HEAD_DIM
    gl = RWKV_GROUP_HEADS * RWKV_HEAD_DIM
    nt = seq // tc
    assert seq % tc == 0 and tc % (RWKV_CHUNK * RWKV_PREP_UNROLL) == 0 and width % gl == 0
    nblk = (tc // RWKV_CHUNK) * (width // gl)
    vec = _resident((1, width))
    lowmat = _resident((RWKV_LOWRANK_PAD, width))
    tile = pltpu.VMEM((tc, width), F32)
    return pl.pallas_call(
        _rwkv_kernel,
        grid=(batch, nt),
        in_specs=[
            pl.BlockSpec((tc, ncols), lambda b, i: (b * nt + i, 0)),
            vec, lowmat, vec, lowmat, lowmat, vec, vec, vec, vec, vec,
        ],
        out_specs=pl.BlockSpec((tc, width), lambda b, i: (b * nt + i, 0)),
        out_shape=jax.ShapeDtypeStruct((n, width), F32),
        scratch_shapes=[
            pltpu.VMEM((width // gl, gl, gl), F32),
            tile, tile, tile, tile, tile, tile, tile, tile, tile,
            pltpu.VMEM((nblk, gl, gl), BF16),
            pltpu.VMEM((nblk, gl, gl), F32),
            pltpu.VMEM((nblk, gl, gl), F32),
        ],
        compiler_params=pltpu.CompilerParams(
            dimension_semantics=("parallel", "arbitrary"), vmem_limit_bytes=VMEM_LIMIT),
        name="rwkv7",
    )(xs, w0, wup, a0, aup, gup, k_k, k_a, r_k, ln_g, ln_b)


def _cummax_rows(x):
    n = x.shape[0]
    row = lax.broadcasted_iota(jnp.int32, x.shape, 0)
    s = 1
    while s < n:
        x = jnp.maximum(x, jnp.where(row >= s, pltpu.roll(x, s, axis=0), -jnp.inf))
        s *= 2
    return x


def _mlstm_kernel(qk_s, p_ref, gbi_ref, gbf_ref, lng_ref, o_ref, cn_ref, m_ref):
    tcm = p_ref.shape[0]
    NH, DH, L = MLSTM_HEADS, MLSTM_HEAD_DIM, MLSTM_CHUNK
    wd = NH * DH
    i = pl.program_id(1)

    @pl.when(i == 0)
    def _():
        cn_ref[...] = jnp.zeros_like(cn_ref)
        m_ref[...] = jnp.full(m_ref.shape, -jnp.inf, F32)

    rr = lax.broadcasted_iota(jnp.int32, (L, L), 0)
    cc = lax.broadcasted_iota(jnp.int32, (L, L), 1)
    causal = cc <= rr
    tri = jnp.where(causal, 1.0, 0.0).astype(BF16)
    eye_l = jnp.where(cc == rr, 1.0, 0.0).astype(BF16)
    eye_h = eye_l[0:2 * SUBLANES, :]
    ones_v = jnp.ones((L, DH), BF16)
    scale = DH ** -0.5
    heads = range(NH)

    def chunk(c):
        sl = slice(c * L, (c + 1) * L)
        li = p_ref[sl, 2 * wd:2 * wd + LANES] + gbi_ref[...]
        lf = -_softplus(-(p_ref[sl, 2 * wd + LANES:2 * wd + 2 * LANES] + gbf_ref[...]))
        b_col = _exact_lhs_dot(tri, lf, 3)
        xcol = li - b_col
        m_prev = m_ref[...]
        zcol = jnp.maximum(m_prev, _cummax_rows(xcol))
        b_end = b_col[L - 1:L, :]
        m_new = jnp.maximum(b_end + m_prev, jnp.max(b_end + xcol, axis=0, keepdims=True))
        keep = jnp.exp(b_end + m_prev - m_new)
        m_ref[...] = m_new
        wshift = b_end - m_new
        xrow = None
        for part in _split_bf16(xcol, 3):
            t = lax.dot_general(eye_h, part, (((1,), (1,)), ((), ())), preferred_element_type=F32)
            xrow = t if xrow is None else xrow + t

        q = [(qk_s[sl, h * DH:(h + 1) * DH] * scale).astype(BF16) for h in heads]
        k = [qk_s[sl, wd + h * DH:wd + (h + 1) * DH].astype(BF16) for h in heads]
        vx = [jnp.concatenate([p_ref[sl, h * DH:(h + 1) * DH].astype(BF16), ones_v], axis=1)
              for h in heads]
        cn = [cn_ref[h] for h in heads]
        s = [lax.dot_general(q[h], k[h], (((1,), (1,)), ((), ())), preferred_element_type=F32) for h in heads]
        kt = [lax.dot_general(eye_l, k[h], (((1,), (1,)), ((), ())), preferred_element_type=F32) for h in heads]
        qc = [jnp.dot(q[h], cn[h].astype(BF16), preferred_element_type=F32) for h in heads]
        zb = [jnp.broadcast_to(zcol[:, h:h + 1], (L, DH)) for h in heads]
        bb = [jnp.broadcast_to(b_col[:, h:h + 1], (L, DH)) for h in heads]
        pw = [(s[h] * jnp.exp(jnp.where(causal, xrow[h:h + 1, :] - zb[h], -jnp.inf))).astype(BF16) for h in heads]
        nd = [jnp.dot(pw[h], vx[h], preferred_element_type=F32) for h in heads]
        for h in heads:
            wrow = jnp.exp(xrow[h:h + 1, :] + wshift[:, h:h + 1])
            cn_ref[h] = keep[:, h:h + 1] * cn[h] + jnp.dot((kt[h] * wrow).astype(BF16), vx[h],
                                                            preferred_element_type=F32)
        for h in heads:
            hs = slice(h * DH, (h + 1) * DH)
            inter = jnp.exp(m_prev[:, h:h + 1] - zb[h])
            num = nd[h][:, :DH] + inter * qc[h][:, :DH]
            den = nd[h][:, DH:] + inter * qc[h][:, DH:]
            hh = num / jnp.maximum(jnp.abs(den), jnp.exp(-(bb[h] + zb[h])))
            hc = hh - jnp.mean(hh, axis=1, keepdims=True)
            hn = hc * lax.rsqrt(jnp.mean(hc * hc, axis=1, keepdims=True) + MLSTM_NORM_EPS)
            og = jax.nn.sigmoid(p_ref[sl, wd + h * DH:wd + (h + 1) * DH])
            o_ref[sl, hs] = og * (hn * lng_ref[:, hs])

    for c in range(tcm // L):
        chunk(c)


def _mlstm_call(qk, p_m, gate_bi, gate_bf, ln_g, *, batch, seq, tcm):
    n, ncols = p_m.shape
    nh, dh = MLSTM_HEADS, MLSTM_HEAD_DIM
    wd = nh * dh
    nt = seq // tcm
    assert seq % tcm == 0 and tcm % MLSTM_CHUNK == 0
    return pl.pallas_call(
        _mlstm_kernel,
        grid=(batch, nt),
        in_specs=[
            pl.BlockSpec((tcm, 2 * wd), lambda b, i: (b * nt + i, 0)),
            pl.BlockSpec((tcm, ncols), lambda b, i: (b * nt + i, 0)),
            _resident((1, LANES)),
            _resident((1, LANES)),
            _resident((1, wd)),
        ],
        out_specs=pl.BlockSpec((tcm, wd), lambda b, i: (b * nt + i, 0)),
        out_shape=jax.ShapeDtypeStruct((n, wd), F32),
        scratch_shapes=[
            pltpu.VMEM((nh, dh, 2 * dh), F32),
            pltpu.VMEM((1, LANES), F32),
        ],
        compiler_params=pltpu.CompilerParams(
            dimension_semantics=("parallel", "arbitrary"), vmem_limit_bytes=VMEM_LIMIT),
        name="mlstm",
    )(qk, p_m, gate_bi, gate_bf, ln_g)


def _mixout_ffn_kernel(x_ref, yr_ref, ym_ref, pg_ref, mod_ref, wr_ref, wm_ref, wo_ref,
                       g_ref, wi_ref, wf_ref, fg_ref, o_ref, *, final_norm):
    dm = x_ref.shape[1]
    zrs = [jnp.dot(y.astype(BF16), wr_ref[...], preferred_element_type=F32) for y in _row_parts(yr_ref, ROW_SPLIT)]
    zms = [jnp.dot(y.astype(BF16), wm_ref[...], preferred_element_type=F32) for y in _row_parts(ym_ref, ROW_SPLIT)]
    mixes = [(jax.nn.sigmoid(pg[:, 0:dm]) * zr + jax.nn.sigmoid(pg[:, dm:2 * dm]) * zm).astype(BF16)
             for pg, zr, zm in zip(_row_parts(pg_ref, ROW_SPLIT), zrs, zms)]
    outs = [jnp.dot(m, wo_ref[...], preferred_element_type=F32) for m in mixes]
    xs = [x + (1.0 + mod_ref[0, 5:6, :]) * o for x, o in zip(_row_parts(x_ref, ROW_SPLIT), outs)]
    _store_row_parts(o_ref, _ffn_half_step(xs, mod_ref, g_ref, wi_ref, wf_ref, fg_ref, mod_row=6,
                                           final_norm=final_norm))


def _mixout_ffn_call(x2, y_r, y_m, p_g, mod_l, w_r, w_m, w_o, norm_g, w_in, w_out, final_g, *,
                     seq, final_norm, tm):
    n, dm = x2.shape
    rowblk = lambda cols: pl.BlockSpec((tm, cols), lambda i: (i, 0))
    kern = functools.partial(_mixout_ffn_kernel, final_norm=final_norm)
    return pl.pallas_call(
        kern,
        grid=(n // tm,),
        in_specs=[
            rowblk(dm), rowblk(y_r.shape[1]), rowblk(y_m.shape[1]), rowblk(p_g.shape[1]),
            pl.BlockSpec((1, 9, dm), lambda i: ((i * tm) // seq, 0, 0)),
            _resident(w_r.shape), _resident(w_m.shape), _resident(w_o.shape),
            _resident((1, dm)), _resident(w_in.shape), _resident(w_out.shape), _resident((1, dm)),
        ],
        out_specs=rowblk(dm),
        out_shape=jax.ShapeDtypeStruct((n, dm), F32),
        compiler_params=pltpu.CompilerParams(
            dimension_semantics=("parallel",), vmem_limit_bytes=VMEM_LIMIT),
        name="mix_out_ffn",
    )(x2, y_r, y_m, p_g, mod_l, w_r, w_m, w_o, norm_g, w_in, w_out, final_g)


def _pad_cols(a, n):
    return jnp.pad(a, ((0, 0), (0, n - a.shape[1])))


def _row(a, n=None):
    a = a.reshape(1, -1)
    return a if n is None else _pad_cols(a, n)


def _lowrank_pad(mat, row0):
    out = jnp.zeros((RWKV_LOWRANK_PAD, mat.shape[1]), F32)
    return lax.dynamic_update_slice(out, mat, (row0, 0)).astype(BF16)


def kernel(x, c, ada_w, ada_b, norm_g, ffn_w_in, ffn_w_out, mix_w_in, rwkv_mu, rwkv_w0, rwkv_w_up, rwkv_a0, rwkv_a_up, rwkv_g_up, rwkv_k_k, rwkv_k_a, rwkv_r_k, rwkv_ln_g, rwkv_ln_b, mlstm_conv_w, mlstm_conv_b, mlstm_gate_b, mlstm_ln_g, branch_w_rwkv, branch_w_mlstm, mix_w_out, final_g):
    batch, seq, dm = x.shape
    depth = ada_w.shape[0]
    n = batch * seq
    rw = RWKV_HEADS * RWKV_HEAD_DIM
    mw = MLSTM_HEADS * MLSTM_HEAD_DIM
    rwkv_cols = 3 * rw + RWKV_DECAY_RANK + RWKV_A_RANK + RWKV_GATE_RANK
    mlstm_cols = 4 * mw + 2 * MLSTM_HEADS
    rwkv_pad = 3 * rw + RWKV_LOWRANK_PAD

    tm_ffn = min(512, seq)
    tm_mix = min(512, seq)
    tc_rwkv = min(512, seq)
    tc_mlstm = min(512, seq)

    c_pad = jnp.pad(c, ((0, SUBLANES - batch % SUBLANES if batch % SUBLANES else 0), (0, 0)))
    mod = _ada_call(c_pad, ada_w, ada_b)[:, :batch].reshape(depth, batch, 9, dm)

    v0 = rwkv_cols + 2 * mw
    gate0 = rwkv_cols + 4 * mw
    lane_pad = lambda a, w: jnp.pad(a, ((0, 0), (0, 0), (0, w - a.shape[-1])))
    mix_w_all = jnp.concatenate([
        mix_w_in[:, :, rwkv_cols:v0],
        lane_pad(mix_w_in[:, :, :rwkv_cols], rwkv_pad),
        mix_w_in[:, :, v0:gate0],
        lane_pad(mix_w_in[:, :, gate0:gate0 + MLSTM_HEADS], LANES),
        lane_pad(mix_w_in[:, :, gate0 + MLSTM_HEADS:gate0 + 2 * MLSTM_HEADS], LANES),
        mix_w_in[:, :, rwkv_cols + mlstm_cols:]], axis=-1).astype(BF16)
    mix_widths = [rwkv_pad, 2 * mw, 2 * mw + 2 * LANES, 2 * dm]

    ffn_w_in_bf = ffn_w_in.astype(BF16)
    ffn_w_out_bf = ffn_w_out.astype(BF16)
    final_row = _row(final_g)

    x2 = x.reshape(n, dm)
    for l in range(depth):
        mod_l = mod[l]
        x2 = _ffn_call(x2, mod_l, _row(norm_g[l, 0]), ffn_w_in_bf[l, 0], ffn_w_out_bf[l, 0], final_row,
                       seq=seq, mod_row=0, final_norm=False, tm=tm_ffn)

        xs_r, qk_m, p_m, p_g = _mixin_call(
            x2, mod_l, _row(norm_g[l, 1]), mix_w_all[l], mix_widths, _row(rwkv_mu[l], rwkv_pad),
            mlstm_conv_w[l], _row(mlstm_conv_b[l]), seq=seq, tm=tm_mix)

        y_r = _rwkv_call(
            xs_r, _row(rwkv_w0[l]),
            _lowrank_pad(rwkv_w_up[l], 0), _row(rwkv_a0[l]),
            _lowrank_pad(rwkv_a_up[l], RWKV_DECAY_RANK),
            _lowrank_pad(rwkv_g_up[l], RWKV_DECAY_RANK + RWKV_A_RANK),
            _row(rwkv_k_k[l]), _row(rwkv_k_a[l]), _row(rwkv_r_k[l]), _row(rwkv_ln_g[l]), _row(rwkv_ln_b[l]),
            batch=batch, seq=seq, tc=tc_rwkv)
        y_m = _mlstm_call(qk_m, p_m,
                          _row(mlstm_gate_b[l, :MLSTM_HEADS], LANES), _row(mlstm_gate_b[l, MLSTM_HEADS:], LANES),
                          _row(mlstm_ln_g[l]), batch=batch, seq=seq, tcm=tc_mlstm)

        x2 = _mixout_ffn_call(
            x2, y_r, y_m, p_g, mod_l, branch_w_rwkv[l].astype(BF16), branch_w_mlstm[l].astype(BF16),
            mix_w_out[l].astype(BF16), _row(norm_g[l, 2]), ffn_w_in_bf[l, 1], ffn_w_out_bf[l, 1], final_row,
            seq=seq, final_norm=(l == depth - 1), tm=tm_ffn)
    return x2.reshape(batch, seq, dm)
```

```python
import functools
import math

import jax
import jax.numpy as jnp
from jax import lax
from jax.experimental import pallas as pl
from jax.experimental.pallas import tpu as pltpu

F32 = jnp.float32
BF16 = jnp.bfloat16

RWKV_HEADS = 8
RWKV_HEAD_DIM = 64
RWKV_DECAY_RANK = 32
RWKV_A_RANK = 32
RWKV_GATE_RANK = 96
MLSTM_HEADS = 4
MLSTM_HEAD_DIM = 128
MLSTM_CONV = 4
MLSTM_CHUNK = 128
RMS_EPS = 1e-6
RWKV_GN_EPS = 64e-5
MLSTM_NORM_EPS = 1e-5
L2_EPS = 1e-12

LANES = 128
SUBLANES = 8
VMEM_LIMIT = 56 * 1024 * 1024
ROW_SPLIT = 2

RWKV_CHUNK = 64
RWKV_GROUP_HEADS = 2
RWKV_PREP_UNROLL = 4
RWKV_LOWRANK_PAD = 256


def _bdot(a, b):
    return jnp.dot(a.astype(BF16), b.astype(BF16), preferred_element_type=F32)


def _bdot_nt(a, b):
    return lax.dot_general(a.astype(BF16), b.astype(BF16), (((1,), (1,)), ((), ())),
                           preferred_element_type=F32)


def _bdot_tn(a, b):
    return lax.dot_general(a.astype(BF16), b.astype(BF16), (((0,), (0,)), ((), ())),
                           preferred_element_type=F32)


def _split_bf16(x, parts):
    out = []
    rem = x
    for _ in range(parts):
        p = rem.astype(BF16)
        out.append(p)
        rem = rem - p.astype(F32)
    return out


def _dot_exact_rhs(x, m_bf16, parts):
    acc = None
    for p in _split_bf16(x, parts):
        t = jnp.dot(p, m_bf16, preferred_element_type=F32)
        acc = t if acc is None else acc + t
    return acc


def _exact_lhs_dot(m_bf16, x, parts):
    acc = None
    for p in _split_bf16(x, parts):
        t = jnp.dot(m_bf16, p, preferred_element_type=F32)
        acc = t if acc is None else acc + t
    return acc


def _rms_mod(x, g, shift, scale):
    y = x * lax.rsqrt(jnp.mean(x * x, axis=-1, keepdims=True) + RMS_EPS) * g
    return y * (1.0 + scale) + shift


def _softplus(z):
    return jnp.maximum(z, 0.0) + jnp.log1p(jnp.exp(-jnp.abs(z)))


def _ada_kernel(c_ref, w_ref, b_ref, o_ref):
    c = c_ref[...]
    cond = c * jax.nn.sigmoid(c)
    o_ref[0] = _bdot(cond, w_ref[0]) + b_ref[0]


def _ada_call(c_pad, ada_w, ada_b):
    depth, dm, n9 = ada_w.shape
    tn = n9 // 4
    rows = c_pad.shape[0]
    return pl.pallas_call(
        _ada_kernel,
        grid=(depth, n9 // tn),
        in_specs=[
            pl.BlockSpec((rows, dm), lambda l, j: (0, 0)),
            pl.BlockSpec((1, dm, tn), lambda l, j: (l, 0, j)),
            pl.BlockSpec((1, 1, tn), lambda l, j: (l, 0, j)),
        ],
        out_specs=pl.BlockSpec((1, rows, tn), lambda l, j: (l, 0, j)),
        out_shape=jax.ShapeDtypeStruct((depth, rows, n9), F32),
        compiler_params=pltpu.CompilerParams(
            dimension_semantics=("arbitrary", "arbitrary"), vmem_limit_bytes=VMEM_LIMIT),
        name="ada_mod",
    )(c_pad, ada_w, ada_b.reshape(depth, 1, n9))


def _row_parts(ref, parts):
    rows = ref.shape[0] // parts
    return [ref[k * rows:(k + 1) * rows, :] for k in range(parts)]


def _store_row_parts(ref, vals):
    rows = ref.shape[0] // len(vals)
    for k, v in enumerate(vals):
        ref[k * rows:(k + 1) * rows, :] = v


def _ffn_half_step(xs, mod_ref, g_ref, wi_ref, wo_ref, fg_ref, *, mod_row, final_norm):
    dff = wo_ref.shape[0]
    shift, scale = mod_ref[0, mod_row:mod_row + 1, :], mod_ref[0, mod_row + 1:mod_row + 2, :]
    hs = [_rms_mod(x, g_ref[...], shift, scale).astype(BF16) for x in xs]
    gus = [jnp.dot(h, wi_ref[...], preferred_element_type=F32) for h in hs]
    acts = [((gu[:, :dff] * jax.nn.sigmoid(gu[:, :dff])) * gu[:, dff:]).astype(BF16) for gu in gus]
    ys = [jnp.dot(a, wo_ref[...], preferred_element_type=F32) for a in acts]
    outs = [x + (0.5 * (1.0 + mod_ref[0, mod_row + 2:mod_row + 3, :])) * y for x, y in zip(xs, ys)]
    if final_norm:
        outs = [o * lax.rsqrt(jnp.mean(o * o, axis=-1, keepdims=True) + RMS_EPS) * fg_ref[...] for o in outs]
    return outs


def _ffn_kernel(x_ref, mod_ref, g_ref, wi_ref, wo_ref, fg_ref, o_ref, *, mod_row, final_norm):
    outs = _ffn_half_step(_row_parts(x_ref, ROW_SPLIT), mod_ref, g_ref, wi_ref, wo_ref, fg_ref,
                          mod_row=mod_row, final_norm=final_norm)
    _store_row_parts(o_ref, outs)


def _resident(shape):
    return pl.BlockSpec(shape, lambda *_: (0,) * len(shape), pipeline_mode=pl.Buffered(1))


def _resident_at(arr, lead):
    tail = arr.shape[len(lead):]
    return pl.BlockSpec((None,) * len(lead) + tail, lambda *_: tuple(lead) + (0,) * len(tail),
                        pipeline_mode=pl.Buffered(1))


def _ffn_call(x2, mod_l, norm_g, w_in, w_out, final_g, *, lead, seq, mod_row, final_norm, tm):
    n, dm = x2.shape
    kern = functools.partial(_ffn_kernel, mod_row=mod_row, final_norm=final_norm)
    return pl.pallas_call(
        kern,
        grid=(n // tm,),
        in_specs=[
            pl.BlockSpec((tm, dm), lambda i: (i, 0)),
            pl.BlockSpec((1, 9, dm), lambda i: ((i * tm) // seq, 0, 0)),
            _resident((1, dm)),
            _resident_at(w_in, lead),
            _resident_at(w_out, lead),
            _resident((1, dm)),
        ],
        out_specs=pl.BlockSpec((tm, dm), lambda i: (i, 0)),
        out_shape=jax.ShapeDtypeStruct((n, dm), F32),
        compiler_params=pltpu.CompilerParams(
            dimension_semantics=("parallel",), vmem_limit_bytes=VMEM_LIMIT),
        name="ffn",
    )(x2, mod_l, norm_g, w_in, w_out, final_g)


def _mixin_kernel(x_ref, mod_ref, g_ref, w_ref, mu_ref, cw_ref, cb_ref,
                  xs_ref, qk_ref, pm_ref, pg_ref, prev_ref, xbuf, *, tiles_per_seq):
    tm = x_ref.shape[0]
    KC = cw_ref.shape[0]
    first = (pl.program_id(0) % tiles_per_seq) == 0
    c1 = qk_ref.shape[1]
    c2 = c1 + xs_ref.shape[1]
    c3 = c2 + pm_ref.shape[1]
    c4 = c3 + pg_ref.shape[1]

    @pl.when(first)
    def _():
        prev_ref[...] = jnp.zeros_like(prev_ref)
        xbuf[0:SUBLANES, :] = jnp.zeros((SUBLANES, xbuf.shape[1]), F32)

    @pl.when(jnp.logical_not(first))
    def _():
        xbuf[0:SUBLANES, :] = xbuf[tm:tm + SUBLANES, :]

    h = _rms_mod(x_ref[...], g_ref[...], mod_ref[0, 3:4, :], mod_ref[0, 4:5, :]).astype(BF16)

    pr = jnp.dot(h, w_ref[:, c1:c2], preferred_element_type=F32)
    row = lax.broadcasted_iota(jnp.int32, (tm, 1), 0)
    prev = jnp.where(row == 0, prev_ref[...], pltpu.roll(pr, 1, axis=0))
    prev_ref[...] = pr[tm - 1:tm, :]
    xs_ref[...] = pr + (prev - pr) * mu_ref[...]

    xbuf[SUBLANES:SUBLANES + tm, :] = jnp.dot(h, w_ref[:, 0:c1], preferred_element_type=F32)
    acc = cb_ref[...] + cw_ref[KC - 1:KC, :] * xbuf[SUBLANES:SUBLANES + tm, :]
    for j in range(KC - 1):
        off = SUBLANES - (KC - 1) + j
        acc = acc + cw_ref[j:j + 1, :] * xbuf[off:off + tm, :]
    qk_ref[...] = acc * jax.nn.sigmoid(acc)

    pm_ref[...] = jnp.dot(h, w_ref[:, c2:c3], preferred_element_type=F32)
    pg_ref[...] = jnp.dot(h, w_ref[:, c3:c4], preferred_element_type=F32)


def _mixin_call(x2, mod_l, norm_g, w_all, widths, mu, conv_w, conv_b, *, layer, seq, tm):
    n, dm = x2.shape
    assert seq % tm == 0 and conv_w.shape[0] - 1 <= SUBLANES and sum(widths) == w_all.shape[-1]
    kern = functools.partial(_mixin_kernel, tiles_per_seq=seq // tm)
    return pl.pallas_call(
        kern,
        grid=(n // tm,),
        in_specs=[
            pl.BlockSpec((tm, dm), lambda i: (i, 0)),
            pl.BlockSpec((1, 9, dm), lambda i: ((i * tm) // seq, 0, 0)),
            _resident((1, dm)),
            _resident_at(w_all, (layer,)),
            _resident(mu.shape), _resident(conv_w.shape), _resident(conv_b.shape),
        ],
        out_specs=[pl.BlockSpec((tm, w), lambda i: (i, 0)) for w in widths],
        out_shape=[jax.ShapeDtypeStruct((n, w), F32) for w in widths],
        scratch_shapes=[pltpu.VMEM((1, widths[0]), F32), pltpu.VMEM((tm + SUBLANES, widths[1]), F32)],
        compiler_params=pltpu.CompilerParams(
            dimension_semantics=("arbitrary",), vmem_limit_bytes=VMEM_LIMIT),
        name="mix_in",
    )(x2, mod_l, norm_g, w_all, mu, conv_w, conv_b)


def _rwkv_kernel(xs_ref, w0_ref, wup_ref, a0_ref, aup_ref, gup_ref, kk_ref, ka_ref, rk_ref,
                 lng_ref, lnb_ref, o_ref,
                 s_ref, r_s, ld_s, k_s, v_s, a_s, b_s, y_s, bonus_s, g_s, mx_s, n0_s, pc_s):
    tc = xs_ref.shape[0]
    width = o_ref.shape[1]
    C = RWKV_CHUNK
    HD = RWKV_HEAD_DIM
    HG = RWKV_GROUP_HEADS
    GL = HG * HD
    n_groups = width // GL
    shift = HD.bit_length() - 1
    i = pl.program_id(1)

    @pl.when(i == 0)
    def _():
        s_ref[...] = jnp.zeros_like(s_ref)

    r = xs_ref[:, 0:width]
    k = xs_ref[:, width:2 * width]
    v = xs_ref[:, 2 * width:3 * width]
    low = xs_ref[:, 3 * width:3 * width + RWKV_LOWRANK_PAD]

    rg = lax.broadcasted_iota(jnp.int32, (GL, GL), 0)
    cg = lax.broadcasted_iota(jnp.int32, (GL, GL), 1)
    blockmask = (rg >> shift) == (cg >> shift)
    eye = rg == cg
    ones_bd = jnp.where(blockmask, 1.0, 0.0).astype(BF16)

    def segsum(x):
        return jnp.concatenate(
            [_dot_exact_rhs(x[:, g * GL:(g + 1) * GL], ones_bd, 2) for g in range(n_groups)], axis=1)

    w = w0_ref[...] + _bdot(jnp.tanh(low), wup_ref[...])
    ld_s[...] = (-math.exp(-0.5)) * jax.nn.sigmoid(w)
    asig = jax.nn.sigmoid(a0_ref[...] + _bdot(low, aup_ref[...]))
    g_s[...] = _bdot(jax.nn.sigmoid(low), gup_ref[...])
    kk = k * kk_ref[...]
    kk = kk * lax.rsqrt(jnp.maximum(segsum(kk * kk), L2_EPS * L2_EPS))
    k2 = k * (1.0 + (asig - 1.0) * ka_ref[...])
    bonus_s[...] = segsum(r * k2 * rk_ref[...]) * v
    r_s[...] = r
    k_s[...] = k2
    v_s[...] = v
    a_s[...] = -kk
    b_s[...] = kk * asig

    lane = lax.broadcasted_iota(jnp.int32, (C, GL), 1)
    tok = lax.broadcasted_iota(jnp.int32, (C, GL), 0)
    strict = (lane & (HD - 1)) < tok
    incl = (lane & (HD - 1)) <= tok
    headmask = [(lane >> shift) == h for h in range(HG)]
    tri = (lax.broadcasted_iota(jnp.int32, (C, C), 1) <= lax.broadcasted_iota(jnp.int32, (C, C), 0))
    tri = jnp.where(tri, 1.0, 0.0).astype(BF16)

    def bd(x):
        return jnp.where(blockmask, jnp.concatenate([x] * HG, axis=0), 0.0).astype(BF16)

    levels = C.bit_length() - 1

    def prepare_stages(c_first):
        inst = []
        for u in range(RWKV_PREP_UNROLL):
            c = c_first + u
            sl = slice(c * C, (c + 1) * C)
            ld = ld_s[sl, :]
            lp = _exact_lhs_dot(tri, ld, 3)
            lpc = lp[C - 1:C, :]
            e_lp = jnp.exp(lp)
            e_nlp = jnp.exp(-lp)
            e_end = jnp.exp(lpc - lp)
            rt = r_s[sl, :] * e_lp
            at = a_s[sl, :] * jnp.exp(lp - ld)
            bb = b_s[sl, :]
            kc = k_s[sl, :]
            bt = bb * e_nlp
            kt = kc * e_nlp
            bh = bb * e_end
            kh = kc * e_end
            vv = v_s[sl, :]
            pc = jnp.exp(lpc)
            for g in range(n_groups):
                gs = slice(g * GL, (g + 1) * GL)
                inst.append(dict(sl=sl, gs=gs, idx=c * n_groups + g, at=at[:, gs], rt=rt[:, gs], v=vv[:, gs],
                                 bt=bt[:, gs], kt=kt[:, gs], bh=bh[:, gs], kh=kh[:, gs], pc=pc[:, gs]))
        for t in inst:
            lhs = jnp.concatenate([t['at'], t['rt']], axis=0)
            rows = [jnp.where(headmask[h], t['bt'], 0.0) for h in range(HG)]
            rows += [jnp.where(headmask[h], t['kt'], 0.0) for h in range(HG)]
            gram = _bdot_nt(lhs, jnp.concatenate(rows, axis=0))
            t['a_ab'] = jnp.where(strict, gram[:C, :GL], 0.0)
            t['a_ak'] = jnp.where(strict, gram[:C, GL:], 0.0)
            t['a_rb'] = jnp.where(incl, gram[C:, :GL], 0.0).astype(BF16)
            t['a_rk'] = jnp.where(incl, gram[C:, GL:], 0.0)
        yield
        for t in inst:
            t['bdv'] = bd(t['v'])
            both = _bdot(jnp.concatenate([t['a_ak'], t['a_rk']], axis=0), t['bdv'])
            t['akv'] = both[:C]
            t['arkv'] = both[C:]
        yield
        for t in inst:
            t['tr'] = t['a_ab']
            t['nm'] = jnp.dot(t['a_ab'].astype(BF16), bd(t['a_ab']), preferred_element_type=F32)
        yield
        for lvl in range(1, levels):
            for t in inst:
                if lvl + 1 < levels:
                    both = _bdot(jnp.concatenate([t['tr'], t['nm']], axis=0), bd(t['nm']))
                    t['tr'] = t['tr'] + t['nm'] + both[:C]
                    t['nm'] = both[C:]
                else:
                    t['tr'] = t['tr'] + t['nm'] + _bdot(t['tr'], bd(t['nm']))
            yield
        for t in inst:
            wu = _bdot(t['tr'], jnp.concatenate([bd(t['at']), bd(t['akv'])], axis=1))
            t['w'] = t['at'] + wu[:, :GL]
            t['u0'] = t['akv'] + wu[:, GL:]
        yield
        for t in inst:
            ry = jnp.dot(t['a_rb'], jnp.concatenate([bd(t['w']), bd(t['u0'])], axis=1),
                         preferred_element_type=F32)
            r_s[t['sl'], t['gs']] = t['rt'] + ry[:, :GL]
            y_s[t['sl'], t['gs']] = ry[:, GL:] + t['arkv']
        yield
        for t in inst:
            mn = _bdot_tn(t['bh'], jnp.concatenate([t['w'], t['u0']], axis=1))
            mx_s[t['idx']] = jnp.where(blockmask, mn[:, :GL], 0.0).astype(BF16)
            n0_s[t['idx']] = jnp.where(blockmask, mn[:, GL:] + _bdot_tn(t['kh'], t['v']), 0.0)
            pc_s[t['idx']] = jnp.broadcast_to(
                jnp.sum(jnp.where(eye, jnp.broadcast_to(t['pc'], (GL, GL)), 0.0), axis=1, keepdims=True),
                (GL, GL))

    def advance(c):
        sl = slice(c * C, (c + 1) * C)
        for g in range(n_groups):
            gs = slice(g * GL, (g + 1) * GL)
            idx = c * n_groups + g
            s_old = s_ref[g]
            s_bf = s_old.astype(BF16)
            y_s[sl, gs] = y_s[sl, gs] + jnp.dot(r_s[sl, gs].astype(BF16), s_bf, preferred_element_type=F32)
            s_ref[g] = pc_s[idx] * s_old + n0_s[idx] + jnp.dot(mx_s[idx], s_bf, preferred_element_type=F32)

    n_chunks = tc // C
    pending = []
    for c_first in range(0, n_chunks, RWKV_PREP_UNROLL):
        stages = prepare_stages(c_first)
        for k, _ in enumerate(stages):
            if pending and k % 2 == 1:
                advance(pending.pop(0))
        for c in pending:
            advance(c)
        pending = list(range(c_first, c_first + RWKV_PREP_UNROLL))
    for c in pending:
        advance(c)

    y = y_s[...]
    inv = 1.0 / HD
    yc = y - segsum(y) * inv
    var = segsum(yc * yc) * inv
    yn = yc * lax.rsqrt(var + RWKV_GN_EPS) * lng_ref[...] + lnb_ref[...]
    o_ref[...] = (yn + bonus_s[...]) * g_s[...]


def _rwkv_call(xs, w0, wup, a0, aup, gup, k_k, k_a, r_k, ln_g, ln_b, *, batch, seq, tc):
    n, ncols = xs.shape
    width = RWKV_HEADS * RWKV_HEAD_DIM
    gl = RWKV_GROUP_HEADS * RWKV_HEAD_DIM
    nt = seq // tc
    assert seq % tc == 0 and tc % (RWKV_CHUNK * RWKV_PREP_UNROLL) == 0 and width % gl == 0
    nblk = (tc // RWKV_CHUNK) * (width // gl)
    vec = _resident((1, width))
    lowmat = _resident((RWKV_LOWRANK_PAD, width))
    tile = pltpu.VMEM((tc, width), F32)
    return pl.pallas_call(
        _rwkv_kernel,
        grid=(batch, nt),
        in_specs=[
            pl.BlockSpec((tc, ncols), lambda b, i: (b * nt + i, 0)),
            vec, lowmat, vec, lowmat, lowmat, vec, vec, vec, vec, vec,
        ],
        out_specs=pl.BlockSpec((tc, width), lambda b, i: (b * nt + i, 0)),
        out_shape=jax.ShapeDtypeStruct((n, width), F32),
        scratch_shapes=[
            pltpu.VMEM((width // gl, gl, gl), F32),
            tile, tile, tile, tile, tile, tile, tile, tile, tile,
            pltpu.VMEM((nblk, gl, gl), BF16),
            pltpu.VMEM((nblk, gl, gl), F32),
            pltpu.VMEM((nblk, gl, gl), F32),
        ],
        compiler_params=pltpu.CompilerParams(
            dimension_semantics=("parallel", "arbitrary"), vmem_limit_bytes=VMEM_LIMIT),
        name="rwkv7",
    )(xs, w0, wup, a0, aup, gup, k_k, k_a, r_k, ln_g, ln_b)


def _cummax_rows(x):
    n = x.shape[0]
    row = lax.broadcasted_iota(jnp.int32, x.shape, 0)
    s = 1
    while s < n:
        x = jnp.maximum(x, jnp.where(row >= s, pltpu.roll(x, s, axis=0), -jnp.inf))
        s *= 2
    return x


def _mlstm_kernel(qk_s, p_ref, gbi_ref, gbf_ref, lng_ref, o_ref, cn_ref, m_ref):
    tcm = p_ref.shape[0]
    NH, DH, L = MLSTM_HEADS, MLSTM_HEAD_DIM, MLSTM_CHUNK
    wd = NH * DH
    i = pl.program_id(1)

    @pl.when(i == 0)
    def _():
        cn_ref[...] = jnp.zeros_like(cn_ref)
        m_ref[...] = jnp.full(m_ref.shape, -jnp.inf, F32)

    rr = lax.broadcasted_iota(jnp.int32, (L, L), 0)
    cc = lax.broadcasted_iota(jnp.int32, (L, L), 1)
    causal = cc <= rr
    tri = jnp.where(causal, 1.0, 0.0).astype(BF16)
    eye_l = jnp.where(cc == rr, 1.0, 0.0).astype(BF16)
    eye_h = eye_l[0:2 * SUBLANES, :]
    ones_v = jnp.ones((L, DH), BF16)
    scale = DH ** -0.5
    heads = range(NH)

    def chunk(c):
        sl = slice(c * L, (c + 1) * L)
        li = p_ref[sl, 2 * wd:2 * wd + LANES] + gbi_ref[...]
        lf = -_softplus(-(p_ref[sl, 2 * wd + LANES:2 * wd + 2 * LANES] + gbf_ref[...]))
        b_col = _exact_lhs_dot(tri, lf, 3)
        xcol = li - b_col
        m_prev = m_ref[...]
        zcol = jnp.maximum(m_prev, _cummax_rows(xcol))
        b_end = b_col[L - 1:L, :]
        m_new = jnp.maximum(b_end + m_prev, jnp.max(b_end + xcol, axis=0, keepdims=True))
        keep = jnp.exp(b_end + m_prev - m_new)
        m_ref[...] = m_new
        wshift = b_end - m_new
        xrow = None
        for part in _split_bf16(xcol, 3):
            t = lax.dot_general(eye_h, part, (((1,), (1,)), ((), ())), preferred_element_type=F32)
            xrow = t if xrow is None else xrow + t

        q = [(qk_s[sl, h * DH:(h + 1) * DH] * scale).astype(BF16) for h in heads]
        k = [qk_s[sl, wd + h * DH:wd + (h + 1) * DH].astype(BF16) for h in heads]
        vx = [jnp.concatenate([p_ref[sl, h * DH:(h + 1) * DH].astype(BF16), ones_v], axis=1)
              for h in heads]
        cn = [cn_ref[h] for h in heads]
        s = [lax.dot_general(q[h], k[h], (((1,), (1,)), ((), ())), preferred_element_type=F32) for h in heads]
        kt = [lax.dot_general(eye_l, k[h], (((1,), (1,)), ((), ())), preferred_element_type=F32) for h in heads]
        qc = [jnp.dot(q[h], cn[h].astype(BF16), preferred_element_type=F32) for h in heads]
        zb = [jnp.broadcast_to(zcol[:, h:h + 1], (L, DH)) for h in heads]
        bb = [jnp.broadcast_to(b_col[:, h:h + 1], (L, DH)) for h in heads]
        pw = [(s[h] * jnp.exp(jnp.where(causal, xrow[h:h + 1, :] - zb[h], -jnp.inf))).astype(BF16) for h in heads]
        nd = [jnp.dot(pw[h], vx[h], preferred_element_type=F32) for h in heads]
        for h in heads:
            wrow = jnp.exp(xrow[h:h + 1, :] + wshift[:, h:h + 1])
            cn_ref[h] = keep[:, h:h + 1] * cn[h] + jnp.dot((kt[h] * wrow).astype(BF16), vx[h],
                                                            preferred_element_type=F32)
        for h in heads:
            hs = slice(h * DH, (h + 1) * DH)
            inter = jnp.exp(m_prev[:, h:h + 1] - zb[h])
            num = nd[h][:, :DH] + inter * qc[h][:, :DH]
            den = nd[h][:, DH:] + inter * qc[h][:, DH:]
            hh = num / jnp.maximum(jnp.abs(den), jnp.exp(-(bb[h] + zb[h])))
            hc = hh - jnp.mean(hh, axis=1, keepdims=True)
            hn = hc * lax.rsqrt(jnp.mean(hc * hc, axis=1, keepdims=True) + MLSTM_NORM_EPS)
            og = jax.nn.sigmoid(p_ref[sl, wd + h * DH:wd + (h + 1) * DH])
            o_ref[sl, hs] = og * (hn * lng_ref[:, hs])

    for c in range(tcm // L):
        chunk(c)


def _mlstm_call(qk, p_m, gate_bi, gate_bf, ln_g, *, batch, seq, tcm):
    n, ncols = p_m.shape
    nh, dh = MLSTM_HEADS, MLSTM_HEAD_DIM
    wd = nh * dh
    nt = seq // tcm
    assert seq % tcm == 0 and tcm % MLSTM_CHUNK == 0
    return pl.pallas_call(
        _mlstm_kernel,
        grid=(batch, nt),
        in_specs=[
            pl.BlockSpec((tcm, 2 * wd), lambda b, i: (b * nt + i, 0)),
            pl.BlockSpec((tcm, ncols), lambda b, i: (b * nt + i, 0)),
            _resident((1, LANES)),
            _resident((1, LANES)),
            _resident((1, wd)),
        ],
        out_specs=pl.BlockSpec((tcm, wd), lambda b, i: (b * nt + i, 0)),
        out_shape=jax.ShapeDtypeStruct((n, wd), F32),
        scratch_shapes=[
            pltpu.VMEM((nh, dh, 2 * dh), F32),
            pltpu.VMEM((1, LANES), F32),
        ],
        compiler_params=pltpu.CompilerParams(
            dimension_semantics=("parallel", "arbitrary"), vmem_limit_bytes=VMEM_LIMIT),
        name="mlstm",
    )(qk, p_m, gate_bi, gate_bf, ln_g)


def _mixout_ffn_kernel(x_ref, yr_ref, ym_ref, pg_ref, mod_ref, wr_ref, wm_ref, wo_ref,
                       g_ref, wi_ref, wf_ref, fg_ref, o_ref, *, final_norm):
    dm = x_ref.shape[1]
    zrs = [jnp.dot(y.astype(BF16), wr_ref[...], preferred_element_type=F32) for y in _row_parts(yr_ref, ROW_SPLIT)]
    zms = [jnp.dot(y.astype(BF16), wm_ref[...], preferred_element_type=F32) for y in _row_parts(ym_ref, ROW_SPLIT)]
    mixes = [(jax.nn.sigmoid(pg[:, 0:dm]) * zr + jax.nn.sigmoid(pg[:, dm:2 * dm]) * zm).astype(BF16)
             for pg, zr, zm in zip(_row_parts(pg_ref, ROW_SPLIT), zrs, zms)]
    outs = [jnp.dot(m, wo_ref[...], preferred_element_type=F32) for m in mixes]
    xs = [x + (1.0 + mod_ref[0, 5:6, :]) * o for x, o in zip(_row_parts(x_ref, ROW_SPLIT), outs)]
    _store_row_parts(o_ref, _ffn_half_step(xs, mod_ref, g_ref, wi_ref, wf_ref, fg_ref, mod_row=6,
                                           final_norm=final_norm))


def _mixout_ffn_call(x2, y_r, y_m, p_g, mod_l, w_r, w_m, w_o, norm_g, w_in, w_out, final_g, *,
                     layer, seq, final_norm, tm):
    n, dm = x2.shape
    rowblk = lambda cols: pl.BlockSpec((tm, cols), lambda i: (i, 0))
    kern = functools.partial(_mixout_ffn_kernel, final_norm=final_norm)
    return pl.pallas_call(
        kern,
        grid=(n // tm,),
        in_specs=[
            rowblk(dm), rowblk(y_r.shape[1]), rowblk(y_m.shape[1]), rowblk(p_g.shape[1]),
            pl.BlockSpec((1, 9, dm), lambda i: ((i * tm) // seq, 0, 0)),
            _resident_at(w_r, (layer,)), _resident_at(w_m, (layer,)), _resident_at(w_o, (layer,)),
            _resident((1, dm)), _resident_at(w_in, (layer, 1)), _resident_at(w_out, (layer, 1)),
            _resident((1, dm)),
        ],
        out_specs=rowblk(dm),
        out_shape=jax.ShapeDtypeStruct((n, dm), F32),
        compiler_params=pltpu.CompilerParams(
            dimension_semantics=("parallel",), vmem_limit_bytes=VMEM_LIMIT),
        name="mix_out_ffn",
    )(x2, y_r, y_m, p_g, mod_l, w_r, w_m, w_o, norm_g, w_in, w_out, final_g)


def _pad_cols(a, n):
    return jnp.pad(a, ((0, 0), (0, n - a.shape[1])))


def _row(a, n=None):
    a = a.reshape(1, -1)
    return a if n is None else _pad_cols(a, n)


def _lowrank_pad(mat, row0):
    out = jnp.zeros((RWKV_LOWRANK_PAD, mat.shape[1]), F32)
    return lax.dynamic_update_slice(out, mat, (row0, 0)).astype(BF16)


def kernel(x, c, ada_w, ada_b, norm_g, ffn_w_in, ffn_w_out, mix_w_in, rwkv_mu, rwkv_w0, rwkv_w_up, rwkv_a0, rwkv_a_up, rwkv_g_up, rwkv_k_k, rwkv_k_a, rwkv_r_k, rwkv_ln_g, rwkv_ln_b, mlstm_conv_w, mlstm_conv_b, mlstm_gate_b, mlstm_ln_g, branch_w_rwkv, branch_w_mlstm, mix_w_out, final_g):
    batch, seq, dm = x.shape
    depth = ada_w.shape[0]
    n = batch * seq
    rw = RWKV_HEADS * RWKV_HEAD_DIM
    mw = MLSTM_HEADS * MLSTM_HEAD_DIM
    rwkv_cols = 3 * rw + RWKV_DECAY_RANK + RWKV_A_RANK + RWKV_GATE_RANK
    mlstm_cols = 4 * mw + 2 * MLSTM_HEADS
    rwkv_pad = 3 * rw + RWKV_LOWRANK_PAD

    tm_ffn = min(512, seq)
    tm_mix = min(512, seq)
    tc_rwkv = min(512, seq)
    tc_mlstm = min(512, seq)

    c_pad = jnp.pad(c, ((0, SUBLANES - batch % SUBLANES if batch % SUBLANES else 0), (0, 0)))
    mod = _ada_call(c_pad, ada_w, ada_b)[:, :batch].reshape(depth, batch, 9, dm)

    v0 = rwkv_cols + 2 * mw
    gate0 = rwkv_cols + 4 * mw
    lane_pad = lambda a, w: jnp.pad(a, ((0, 0), (0, 0), (0, w - a.shape[-1])))
    mix_w_all = jnp.concatenate([
        mix_w_in[:, :, rwkv_cols:v0],
        lane_pad(mix_w_in[:, :, :rwkv_cols], rwkv_pad),
        mix_w_in[:, :, v0:gate0],
        lane_pad(mix_w_in[:, :, gate0:gate0 + MLSTM_HEADS], LANES),
        lane_pad(mix_w_in[:, :, gate0 + MLSTM_HEADS:gate0 + 2 * MLSTM_HEADS], LANES),
        mix_w_in[:, :, rwkv_cols + mlstm_cols:]], axis=-1).astype(BF16)
    mix_widths = [rwkv_pad, 2 * mw, 2 * mw + 2 * LANES, 2 * dm]

    ffn_w_in_bf = ffn_w_in.astype(BF16)
    ffn_w_out_bf = ffn_w_out.astype(BF16)
    branch_r_bf = branch_w_rwkv.astype(BF16)
    branch_m_bf = branch_w_mlstm.astype(BF16)
    mix_w_out_bf = mix_w_out.astype(BF16)
    final_row = _row(final_g)

    x2 = x.reshape(n, dm)
    for l in range(depth):
        mod_l = mod[l]
        x2 = _ffn_call(x2, mod_l, _row(norm_g[l, 0]), ffn_w_in_bf, ffn_w_out_bf, final_row,
                       lead=(l, 0), seq=seq, mod_row=0, final_norm=False, tm=tm_ffn)

        xs_r, qk_m, p_m, p_g = _mixin_call(
            x2, mod_l, _row(norm_g[l, 1]), mix_w_all, mix_widths, _row(rwkv_mu[l], rwkv_pad),
            mlstm_conv_w[l], _row(mlstm_conv_b[l]), layer=l, seq=seq, tm=tm_mix)

        y_r = _rwkv_call(
            xs_r, _row(rwkv_w0[l]),
            _lowrank_pad(rwkv_w_up[l], 0), _row(rwkv_a0[l]),
            _lowrank_pad(rwkv_a_up[l], RWKV_DECAY_RANK),
            _lowrank_pad(rwkv_g_up[l], RWKV_DECAY_RANK + RWKV_A_RANK),
            _row(rwkv_k_k[l]), _row(rwkv_k_a[l]), _row(rwkv_r_k[l]), _row(rwkv_ln_g[l]), _row(rwkv_ln_b[l]),
            batch=batch, seq=seq, tc=tc_rwkv)
        y_m = _mlstm_call(qk_m, p_m,
                          _row(mlstm_gate_b[l, :MLSTM_HEADS], LANES), _row(mlstm_gate_b[l, MLSTM_HEADS:], LANES),
                          _row(mlstm_ln_g[l]), batch=batch, seq=seq, tcm=tc_mlstm)

        x2 = _mixout_ffn_call(
            x2, y_r, y_m, p_g, mod_l, branch_r_bf, branch_m_bf, mix_w_out_bf, _row(norm_g[l, 2]),
            ffn_w_in_bf, ffn_w_out_bf, final_row, layer=l, seq=seq, final_norm=(l == depth - 1), tm=tm_ffn)
    return x2.reshape(batch, seq, dm)
```

```python
import functools
import math

import jax
import jax.numpy as jnp
from jax import lax
from jax.experimental import pallas as pl
from jax.experimental.pallas import tpu as pltpu

F32 = jnp.float32
BF16 = jnp.bfloat16

RWKV_HEADS = 8
RWKV_HEAD_DIM = 64
RWKV_DECAY_RANK = 32
RWKV_A_RANK = 32
RWKV_GATE_RANK = 96
MLSTM_HEADS = 4
MLSTM_HEAD_DIM = 128
MLSTM_CONV = 4
MLSTM_CHUNK = 128
RMS_EPS = 1e-6
RWKV_GN_EPS = 64e-5
MLSTM_NORM_EPS = 1e-5
L2_EPS = 1e-12

LANES = 128
SUBLANES = 8
VMEM_LIMIT = 56 * 1024 * 1024
ROW_SPLIT = 2

RWKV_CHUNK = 64
RWKV_GROUP_HEADS = 2
RWKV_PREP_UNROLL = 4
RWKV_LOWRANK_PAD = 256


def _bdot(a, b):
    return jnp.dot(a.astype(BF16), b.astype(BF16), preferred_element_type=F32)


def _bdot_nt(a, b):
    return lax.dot_general(a.astype(BF16), b.astype(BF16), (((1,), (1,)), ((), ())),
                           preferred_element_type=F32)


def _bdot_tn(a, b):
    return lax.dot_general(a.astype(BF16), b.astype(BF16), (((0,), (0,)), ((), ())),
                           preferred_element_type=F32)


def _split_bf16(x, parts):
    out = []
    rem = x
    for _ in range(parts):
        p = rem.astype(BF16)
        out.append(p)
        rem = rem - p.astype(F32)
    return out


def _dot_exact_rhs(x, m_bf16, parts):
    acc = None
    for p in _split_bf16(x, parts):
        t = jnp.dot(p, m_bf16, preferred_element_type=F32)
        acc = t if acc is None else acc + t
    return acc


def _exact_lhs_dot(m_bf16, x, parts):
    acc = None
    for p in _split_bf16(x, parts):
        t = jnp.dot(m_bf16, p, preferred_element_type=F32)
        acc = t if acc is None else acc + t
    return acc


def _rms_mod(x, g, shift, scale):
    y = x * lax.rsqrt(jnp.mean(x * x, axis=-1, keepdims=True) + RMS_EPS) * g
    return y * (1.0 + scale) + shift


def _softplus(z):
    return jnp.maximum(z, 0.0) + jnp.log1p(jnp.exp(-jnp.abs(z)))


def _ada_kernel(c_ref, w_ref, b_ref, o_ref):
    c = c_ref[...]
    cond = c * jax.nn.sigmoid(c)
    o_ref[0] = _bdot(cond, w_ref[0]) + b_ref[0]


def _ada_call(c_pad, ada_w, ada_b):
    depth, dm, n9 = ada_w.shape
    tn = n9 // 4
    rows = c_pad.shape[0]
    return pl.pallas_call(
        _ada_kernel,
        grid=(depth, n9 // tn),
        in_specs=[
            pl.BlockSpec((rows, dm), lambda l, j: (0, 0)),
            pl.BlockSpec((1, dm, tn), lambda l, j: (l, 0, j)),
            pl.BlockSpec((1, 1, tn), lambda l, j: (l, 0, j)),
        ],
        out_specs=pl.BlockSpec((1, rows, tn), lambda l, j: (l, 0, j)),
        out_shape=jax.ShapeDtypeStruct((depth, rows, n9), F32),
        compiler_params=pltpu.CompilerParams(
            dimension_semantics=("arbitrary", "arbitrary"), vmem_limit_bytes=VMEM_LIMIT),
        name="ada_mod",
    )(c_pad, ada_w, ada_b.reshape(depth, 1, n9))


def _row_parts(ref, parts):
    rows = ref.shape[0] // parts
    return [ref[k * rows:(k + 1) * rows, :] for k in range(parts)]


def _store_row_parts(ref, vals):
    rows = ref.shape[0] // len(vals)
    for k, v in enumerate(vals):
        ref[k * rows:(k + 1) * rows, :] = v


def _ffn_half_step(xs, mod_ref, g_ref, wi_ref, wo_ref, fg_ref, *, mod_row, final_norm):
    dff = wo_ref.shape[0]
    shift, scale = mod_ref[0, mod_row:mod_row + 1, :], mod_ref[0, mod_row + 1:mod_row + 2, :]
    hs = [_rms_mod(x, g_ref[...], shift, scale).astype(BF16) for x in xs]
    gus = [jnp.dot(h, wi_ref[...], preferred_element_type=F32) for h in hs]
    acts = [((gu[:, :dff] * jax.nn.sigmoid(gu[:, :dff])) * gu[:, dff:]).astype(BF16) for gu in gus]
    ys = [jnp.dot(a, wo_ref[...], preferred_element_type=F32) for a in acts]
    outs = [x + (0.5 * (1.0 + mod_ref[0, mod_row + 2:mod_row + 3, :])) * y for x, y in zip(xs, ys)]
    if final_norm:
        outs = [o * lax.rsqrt(jnp.mean(o * o, axis=-1, keepdims=True) + RMS_EPS) * fg_ref[...] for o in outs]
    return outs


def _ffn_kernel(x_ref, mod_ref, g_ref, wi_ref, wo_ref, fg_ref, o_ref, *, mod_row, final_norm):
    outs = _ffn_half_step(_row_parts(x_ref, ROW_SPLIT), mod_ref, g_ref, wi_ref, wo_ref, fg_ref,
                          mod_row=mod_row, final_norm=final_norm)
    _store_row_parts(o_ref, outs)


def _resident(shape):
    return pl.BlockSpec(shape, lambda *_: (0,) * len(shape), pipeline_mode=pl.Buffered(1))


def _resident_at(arr, lead):
    tail = arr.shape[len(lead):]
    return pl.BlockSpec((None,) * len(lead) + tail, lambda *_: tuple(lead) + (0,) * len(tail),
                        pipeline_mode=pl.Buffered(1))


def _ffn_call(x2, mod_l, norm_g, w_in, w_out, final_g, *, lead, seq, mod_row, final_norm, tm):
    n, dm = x2.shape
    kern = functools.partial(_ffn_kernel, mod_row=mod_row, final_norm=final_norm)
    return pl.pallas_call(
        kern,
        grid=(n // tm,),
        in_specs=[
            pl.BlockSpec((tm, dm), lambda i: (i, 0)),
            pl.BlockSpec((1, 9, dm), lambda i: ((i * tm) // seq, 0, 0)),
            _resident((1, dm)),
            _resident_at(w_in, lead),
            _resident_at(w_out, lead),
            _resident((1, dm)),
        ],
        out_specs=pl.BlockSpec((tm, dm), lambda i: (i, 0)),
        out_shape=jax.ShapeDtypeStruct((n, dm), F32),
        compiler_params=pltpu.CompilerParams(
            dimension_semantics=("parallel",), vmem_limit_bytes=VMEM_LIMIT),
        name="ffn",
    )(x2, mod_l, norm_g, w_in, w_out, final_g)


def _mixin_kernel(x_ref, mod_ref, g_ref, w_ref, mu_ref, cw_ref, cb_ref,
                  xs_ref, qk_ref, pm_ref, pg_ref, prev_ref, xbuf, *, tiles_per_seq):
    tm = x_ref.shape[0]
    KC = cw_ref.shape[0]
    first = (pl.program_id(0) % tiles_per_seq) == 0
    c1 = qk_ref.shape[1]
    c2 = c1 + xs_ref.shape[1]
    c3 = c2 + pm_ref.shape[1]
    c4 = c3 + pg_ref.shape[1]

    @pl.when(first)
    def _():
        prev_ref[...] = jnp.zeros_like(prev_ref)
        xbuf[0:SUBLANES, :] = jnp.zeros((SUBLANES, xbuf.shape[1]), F32)

    @pl.when(jnp.logical_not(first))
    def _():
        xbuf[0:SUBLANES, :] = xbuf[tm:tm + SUBLANES, :]

    h = _rms_mod(x_ref[...], g_ref[...], mod_ref[0, 3:4, :], mod_ref[0, 4:5, :]).astype(BF16)

    pr = jnp.dot(h, w_ref[:, c1:c2], preferred_element_type=F32)
    row = lax.broadcasted_iota(jnp.int32, (tm, 1), 0)
    prev = jnp.where(row == 0, prev_ref[...], pltpu.roll(pr, 1, axis=0))
    prev_ref[...] = pr[tm - 1:tm, :]
    xs_ref[...] = pr + (prev - pr) * mu_ref[...]

    xbuf[SUBLANES:SUBLANES + tm, :] = jnp.dot(h, w_ref[:, 0:c1], preferred_element_type=F32)
    acc = cb_ref[...] + cw_ref[KC - 1:KC, :] * xbuf[SUBLANES:SUBLANES + tm, :]
    for j in range(KC - 1):
        off = SUBLANES - (KC - 1) + j
        acc = acc + cw_ref[j:j + 1, :] * xbuf[off:off + tm, :]
    qk_ref[...] = acc * jax.nn.sigmoid(acc)

    pm_ref[...] = jnp.dot(h, w_ref[:, c2:c3], preferred_element_type=F32)
    pg_ref[...] = jnp.dot(h, w_ref[:, c3:c4], preferred_element_type=F32)


def _mixin_call(x2, mod_l, norm_g, w_all, widths, mu, conv_w, conv_b, *, layer, seq, tm):
    n, dm = x2.shape
    assert seq % tm == 0 and conv_w.shape[0] - 1 <= SUBLANES and sum(widths) == w_all.shape[-1]
    kern = functools.partial(_mixin_kernel, tiles_per_seq=seq // tm)
    return pl.pallas_call(
        kern,
        grid=(n // tm,),
        in_specs=[
            pl.BlockSpec((tm, dm), lambda i: (i, 0)),
            pl.BlockSpec((1, 9, dm), lambda i: ((i * tm) // seq, 0, 0)),
            _resident((1, dm)),
            _resident_at(w_all, (layer,)),
            _resident(mu.shape), _resident(conv_w.shape), _resident(conv_b.shape),
        ],
        out_specs=[pl.BlockSpec((tm, w), lambda i: (i, 0)) for w in widths],
        out_shape=[jax.ShapeDtypeStruct((n, w), F32) for w in widths],
        scratch_shapes=[pltpu.VMEM((1, widths[0]), F32), pltpu.VMEM((tm + SUBLANES, widths[1]), F32)],
        compiler_params=pltpu.CompilerParams(
            dimension_semantics=("arbitrary",), vmem_limit_bytes=VMEM_LIMIT),
        name="mix_in",
    )(x2, mod_l, norm_g, w_all, mu, conv_w, conv_b)


def _rwkv_kernel(xs_ref, w0_ref, wup_ref, a0_ref, aup_ref, gup_ref, kk_ref, ka_ref, rk_ref,
                 lng_ref, lnb_ref, o_ref,
                 s_ref, r_s, ld_s, k_s, v_s, a_s, b_s, y_s, bonus_s, g_s, mx_s, n0_s, pc_s):
    tc = xs_ref.shape[0]
    width = o_ref.shape[1]
    C = RWKV_CHUNK
    HD = RWKV_HEAD_DIM
    HG = RWKV_GROUP_HEADS
    GL = HG * HD
    n_groups = width // GL
    shift = HD.bit_length() - 1
    i = pl.program_id(1)

    @pl.when(i == 0)
    def _():
        s_ref[...] = jnp.zeros_like(s_ref)

    r = xs_ref[:, 0:width]
    k = xs_ref[:, width:2 * width]
    v = xs_ref[:, 2 * width:3 * width]
    low = xs_ref[:, 3 * width:3 * width + RWKV_LOWRANK_PAD]

    rg = lax.broadcasted_iota(jnp.int32, (GL, GL), 0)
    cg = lax.broadcasted_iota(jnp.int32, (GL, GL), 1)
    blockmask = (rg >> shift) == (cg >> shift)
    eye = rg == cg
    ones_bd = jnp.where(blockmask, 1.0, 0.0).astype(BF16)

    def segsum(x):
        return jnp.concatenate(
            [_dot_exact_rhs(x[:, g * GL:(g + 1) * GL], ones_bd, 2) for g in range(n_groups)], axis=1)

    w = w0_ref[...] + _bdot(jnp.tanh(low), wup_ref[...])
    ld_s[...] = (-math.exp(-0.5)) * jax.nn.sigmoid(w)
    asig = jax.nn.sigmoid(a0_ref[...] + _bdot(low, aup_ref[...]))
    g_s[...] = _bdot(jax.nn.sigmoid(low), gup_ref[...])
    kk = k * kk_ref[...]
    kk = kk * lax.rsqrt(jnp.maximum(segsum(kk * kk), L2_EPS * L2_EPS))
    k2 = k * (1.0 + (asig - 1.0) * ka_ref[...])
    bonus_s[...] = segsum(r * k2 * rk_ref[...]) * v
    r_s[...] = r
    k_s[...] = k2
    v_s[...] = v
    a_s[...] = -kk
    b_s[...] = kk * asig

    lane = lax.broadcasted_iota(jnp.int32, (C, GL), 1)
    tok = lax.broadcasted_iota(jnp.int32, (C, GL), 0)
    strict = (lane & (HD - 1)) < tok
    incl = (lane & (HD - 1)) <= tok
    headmask = [(lane >> shift) == h for h in range(HG)]
    tri = (lax.broadcasted_iota(jnp.int32, (C, C), 1) <= lax.broadcasted_iota(jnp.int32, (C, C), 0))
    tri = jnp.where(tri, 1.0, 0.0).astype(BF16)

    def bd(x):
        return jnp.where(blockmask, jnp.concatenate([x] * HG, axis=0), 0.0).astype(BF16)

    levels = C.bit_length() - 1

    def prepare_stages(c_first):
        inst = []
        for u in range(RWKV_PREP_UNROLL):
            c = c_first + u
            sl = slice(c * C, (c + 1) * C)
            ld = ld_s[sl, :]
            lp = _exact_lhs_dot(tri, ld, 3)
            lpc = lp[C - 1:C, :]
            e_lp = jnp.exp(lp)
            e_nlp = jnp.exp(-lp)
            e_end = jnp.exp(lpc - lp)
            rt = r_s[sl, :] * e_lp
            at = a_s[sl, :] * jnp.exp(lp - ld)
            bb = b_s[sl, :]
            kc = k_s[sl, :]
            bt = bb * e_nlp
            kt = kc * e_nlp
            bh = bb * e_end
            kh = kc * e_end
            vv = v_s[sl, :]
            pc = jnp.exp(lpc)
            for g in range(n_groups):
                gs = slice(g * GL, (g + 1) * GL)
                inst.append(dict(sl=sl, gs=gs, idx=c * n_groups + g, at=at[:, gs], rt=rt[:, gs], v=vv[:, gs],
                                 bt=bt[:, gs], kt=kt[:, gs], bh=bh[:, gs], kh=kh[:, gs], pc=pc[:, gs]))
        for t in inst:
            lhs = jnp.concatenate([t['at'], t['rt']], axis=0)
            rows = [jnp.where(headmask[h], t['bt'], 0.0) for h in range(HG)]
            rows += [jnp.where(headmask[h], t['kt'], 0.0) for h in range(HG)]
            gram = _bdot_nt(lhs, jnp.concatenate(rows, axis=0))
            t['a_ab'] = jnp.where(strict, gram[:C, :GL], 0.0)
            t['a_ak'] = jnp.where(strict, gram[:C, GL:], 0.0)
            t['a_rb'] = jnp.where(incl, gram[C:, :GL], 0.0).astype(BF16)
            t['a_rk'] = jnp.where(incl, gram[C:, GL:], 0.0)
        yield
        for t in inst:
            t['bdv'] = bd(t['v'])
            both = _bdot(jnp.concatenate([t['a_ak'], t['a_rk']], axis=0), t['bdv'])
            t['akv'] = both[:C]
            t['arkv'] = both[C:]
        yield
        for t in inst:
            t['tr'] = t['a_ab']
            t['nm'] = jnp.dot(t['a_ab'].astype(BF16), bd(t['a_ab']), preferred_element_type=F32)
        yield
        for lvl in range(1, levels):
            for t in inst:
                if lvl + 1 < levels:
                    both = _bdot(jnp.concatenate([t['tr'], t['nm']], axis=0), bd(t['nm']))
                    t['tr'] = t['tr'] + t['nm'] + both[:C]
                    t['nm'] = both[C:]
                else:
                    t['tr'] = t['tr'] + t['nm'] + _bdot(t['tr'], bd(t['nm']))
            yield
        for t in inst:
            wu = _bdot(t['tr'], jnp.concatenate([bd(t['at']), bd(t['akv'])], axis=1))
            t['w'] = t['at'] + wu[:, :GL]
            t['u0'] = t['akv'] + wu[:, GL:]
        yield
        for t in inst:
            ry = jnp.dot(t['a_rb'], jnp.concatenate([bd(t['w']), bd(t['u0'])], axis=1),
                         preferred_element_type=F32)
            r_s[t['sl'], t['gs']] = t['rt'] + ry[:, :GL]
            y_s[t['sl'], t['gs']] = ry[:, GL:] + t['arkv']
        yield
        for t in inst:
            mn = _bdot_tn(t['bh'], jnp.concatenate([t['w'], t['u0']], axis=1))
            mx_s[t['idx']] = jnp.where(blockmask, mn[:, :GL], 0.0).astype(BF16)
            n0_s[t['idx']] = jnp.where(blockmask, mn[:, GL:] + _bdot_tn(t['kh'], t['v']), 0.0)
            pc_s[t['idx']] = jnp.broadcast_to(
                jnp.sum(jnp.where(eye, jnp.broadcast_to(t['pc'], (GL, GL)), 0.0), axis=1, keepdims=True),
                (GL, GL))

    def advance(c):
        sl = slice(c * C, (c + 1) * C)
        for g in range(n_groups):
            gs = slice(g * GL, (g + 1) * GL)
            idx = c * n_groups + g
            s_old = s_ref[g]
            s_bf = s_old.astype(BF16)
            y_s[sl, gs] = y_s[sl, gs] + jnp.dot(r_s[sl, gs].astype(BF16), s_bf, preferred_element_type=F32)
            s_ref[g] = pc_s[idx] * s_old + n0_s[idx] + jnp.dot(mx_s[idx], s_bf, preferred_element_type=F32)

    n_chunks = tc // C
    pending = []
    for c_first in range(0, n_chunks, RWKV_PREP_UNROLL):
        stages = prepare_stages(c_first)
        for k, _ in enumerate(stages):
            if pending and k % 2 == 1:
                advance(pending.pop(0))
        for c in pending:
            advance(c)
        pending = list(range(c_first, c_first + RWKV_PREP_UNROLL))
    for c in pending:
        advance(c)

    y = y_s[...]
    inv = 1.0 / HD
    yc = y - segsum(y) * inv
    var = segsum(yc * yc) * inv
    yn = yc * lax.rsqrt(var + RWKV_GN_EPS) * lng_ref[...] + lnb_ref[...]
    o_ref[...] = (yn + bonus_s[...]) * g_s[...]


BF16_SUBLANES = 16


def _cast_blocking(rows, cols, steps):
    for cb in range(1, steps + 1):
        rb = steps // cb
        if (steps % cb == 0 and rows % rb == 0 and cols % cb == 0
                and (rows // rb) % BF16_SUBLANES == 0 and (cols // cb) % LANES == 0):
            return rb, cb
    raise ValueError((rows, cols, steps))


def _cast_rider_specs(casts, nt):
    in_specs, out_specs, out_shapes = [], [], []
    for arr, lead, (rb, cb) in casts:
        rows, cols = arr.shape[-2:]
        blk = (rows // rb, cols // cb)
        in_specs.append(pl.BlockSpec(
            (None,) * len(lead) + blk,
            lambda b, i, lead=lead, cb=cb: tuple(lead) + ((b * nt + i) // cb, (b * nt + i) % cb)))
        out_specs.append(pl.BlockSpec(blk, lambda b, i, cb=cb: ((b * nt + i) // cb, (b * nt + i) % cb)))
        out_shapes.append(jax.ShapeDtypeStruct((rows, cols), BF16))
    return in_specs, out_specs, out_shapes


def _with_cast_riders(body, n_in, n_out, n_cast):
    def kern(*refs):
        ins = refs[:n_in]
        cast_ins = refs[n_in:n_in + n_cast]
        outs = refs[n_in + n_cast:n_in + n_cast + n_out]
        cast_outs = refs[n_in + n_cast + n_out:n_in + 2 * n_cast + n_out]
        scratch = refs[n_in + 2 * n_cast + n_out:]
        for src, dst in zip(cast_ins, cast_outs):
            dst[...] = src[...].astype(BF16)
        body(*ins, *outs, *scratch)
    return kern


def _rwkv_call(xs, w0, wup, a0, aup, gup, k_k, k_a, r_k, ln_g, ln_b, *, batch, seq, tc, casts=()):
    n, ncols = xs.shape
    width = RWKV_HEADS * RWKV_HEAD_DIM
    gl = RWKV_GROUP_HEADS * RWKV_HEAD_DIM
    nt = seq // tc
    assert seq % tc == 0 and tc % (RWKV_CHUNK * RWKV_PREP_UNROLL) == 0 and width % gl == 0
    nblk = (tc // RWKV_CHUNK) * (width // gl)
    vec = _resident((1, width))
    lowmat = _resident((RWKV_LOWRANK_PAD, width))
    tile = pltpu.VMEM((tc, width), F32)
    casts = [(arr, lead, _cast_blocking(*arr.shape[-2:], batch * nt)) for arr, lead in casts]
    cast_in, cast_out, cast_shapes = _cast_rider_specs(casts, nt)
    in_specs = [pl.BlockSpec((tc, ncols), lambda b, i: (b * nt + i, 0)),
                vec, lowmat, vec, lowmat, lowmat, vec, vec, vec, vec, vec]
    outs = pl.pallas_call(
        _with_cast_riders(_rwkv_kernel, len(in_specs), 1, len(casts)),
        grid=(batch, nt),
        in_specs=in_specs + cast_in,
        out_specs=[pl.BlockSpec((tc, width), lambda b, i: (b * nt + i, 0))] + cast_out,
        out_shape=[jax.ShapeDtypeStruct((n, width), F32)] + cast_shapes,
        scratch_shapes=[
            pltpu.VMEM((width // gl, gl, gl), F32),
            tile, tile, tile, tile, tile, tile, tile, tile, tile,
            pltpu.VMEM((nblk, gl, gl), BF16),
            pltpu.VMEM((nblk, gl, gl), F32),
            pltpu.VMEM((nblk, gl, gl), F32),
        ],
        compiler_params=pltpu.CompilerParams(
            dimension_semantics=("parallel", "arbitrary"), vmem_limit_bytes=VMEM_LIMIT),
        name="rwkv7",
    )(xs, w0, wup, a0, aup, gup, k_k, k_a, r_k, ln_g, ln_b, *[c[0] for c in casts])
    return outs[0], list(outs[1:])


def _cummax_rows(x):
    n = x.shape[0]
    row = lax.broadcasted_iota(jnp.int32, x.shape, 0)
    s = 1
    while s < n:
        x = jnp.maximum(x, jnp.where(row >= s, pltpu.roll(x, s, axis=0), -jnp.inf))
        s *= 2
    return x


def _mlstm_kernel(qk_s, p_ref, gbi_ref, gbf_ref, lng_ref, o_ref, cn_ref, m_ref):
    tcm = p_ref.shape[0]
    NH, DH, L = MLSTM_HEADS, MLSTM_HEAD_DIM, MLSTM_CHUNK
    wd = NH * DH
    i = pl.program_id(1)

    @pl.when(i == 0)
    def _():
        cn_ref[...] = jnp.zeros_like(cn_ref)
        m_ref[...] = jnp.full(m_ref.shape, -jnp.inf, F32)

    rr = lax.broadcasted_iota(jnp.int32, (L, L), 0)
    cc = lax.broadcasted_iota(jnp.int32, (L, L), 1)
    causal = cc <= rr
    tri = jnp.where(causal, 1.0, 0.0).astype(BF16)
    eye_l = jnp.where(cc == rr, 1.0, 0.0).astype(BF16)
    eye_h = eye_l[0:2 * SUBLANES, :]
    ones_v = jnp.ones((L, DH), BF16)
    scale = DH ** -0.5
    heads = range(NH)

    def chunk(c):
        sl = slice(c * L, (c + 1) * L)
        li = p_ref[sl, 2 * wd:2 * wd + LANES] + gbi_ref[...]
        lf = -_softplus(-(p_ref[sl, 2 * wd + LANES:2 * wd + 2 * LANES] + gbf_ref[...]))
        b_col = _exact_lhs_dot(tri, lf, 3)
        xcol = li - b_col
        m_prev = m_ref[...]
        zcol = jnp.maximum(m_prev, _cummax_rows(xcol))
        b_end = b_col[L - 1:L, :]
        m_new = jnp.maximum(b_end + m_prev, jnp.max(b_end + xcol, axis=0, keepdims=True))
        keep = jnp.exp(b_end + m_prev - m_new)
        m_ref[...] = m_new
        wshift = b_end - m_new
        xrow = None
        for part in _split_bf16(xcol, 3):
            t = lax.dot_general(eye_h, part, (((1,), (1,)), ((), ())), preferred_element_type=F32)
            xrow = t if xrow is None else xrow + t

        q = [(qk_s[sl, h * DH:(h + 1) * DH] * scale).astype(BF16) for h in heads]
        k = [qk_s[sl, wd + h * DH:wd + (h + 1) * DH].astype(BF16) for h in heads]
        vx = [jnp.concatenate([p_ref[sl, h * DH:(h + 1) * DH].astype(BF16), ones_v], axis=1)
              for h in heads]
        cn = [cn_ref[h] for h in heads]
        s = [lax.dot_general(q[h], k[h], (((1,), (1,)), ((), ())), preferred_element_type=F32) for h in heads]
        kt = [lax.dot_general(eye_l, k[h], (((1,), (1,)), ((), ())), preferred_element_type=F32) for h in heads]
        qc = [jnp.dot(q[h], cn[h].astype(BF16), preferred_element_type=F32) for h in heads]
        zb = [jnp.broadcast_to(zcol[:, h:h + 1], (L, DH)) for h in heads]
        bb = [jnp.broadcast_to(b_col[:, h:h + 1], (L, DH)) for h in heads]
        pw = [(s[h] * jnp.exp(jnp.where(causal, xrow[h:h + 1, :] - zb[h], -jnp.inf))).astype(BF16) for h in heads]
        nd = [jnp.dot(pw[h], vx[h], preferred_element_type=F32) for h in heads]
        for h in heads:
            wrow = jnp.exp(xrow[h:h + 1, :] + wshift[:, h:h + 1])
            cn_ref[h] = keep[:, h:h + 1] * cn[h] + jnp.dot((kt[h] * wrow).astype(BF16), vx[h],
                                                            preferred_element_type=F32)
        for h in heads:
            hs = slice(h * DH, (h + 1) * DH)
            inter = jnp.exp(m_prev[:, h:h + 1] - zb[h])
            num = nd[h][:, :DH] + inter * qc[h][:, :DH]
            den = nd[h][:, DH:] + inter * qc[h][:, DH:]
            hh = num / jnp.maximum(jnp.abs(den), jnp.exp(-(bb[h] + zb[h])))
            hc = hh - jnp.mean(hh, axis=1, keepdims=True)
            hn = hc * lax.rsqrt(jnp.mean(hc * hc, axis=1, keepdims=True) + MLSTM_NORM_EPS)
            og = jax.nn.sigmoid(p_ref[sl, wd + h * DH:wd + (h + 1) * DH])
            o_ref[sl, hs] = og * (hn * lng_ref[:, hs])

    for c in range(tcm // L):
        chunk(c)


def _mlstm_call(qk, p_m, gate_bi, gate_bf, ln_g, *, batch, seq, tcm, casts=()):
    n, ncols = p_m.shape
    nh, dh = MLSTM_HEADS, MLSTM_HEAD_DIM
    wd = nh * dh
    nt = seq // tcm
    assert seq % tcm == 0 and tcm % MLSTM_CHUNK == 0
    casts = [(arr, lead, _cast_blocking(*arr.shape[-2:], batch * nt)) for arr, lead in casts]
    cast_in, cast_out, cast_shapes = _cast_rider_specs(casts, nt)
    in_specs = [
        pl.BlockSpec((tcm, 2 * wd), lambda b, i: (b * nt + i, 0)),
        pl.BlockSpec((tcm, ncols), lambda b, i: (b * nt + i, 0)),
        _resident((1, LANES)),
        _resident((1, LANES)),
        _resident((1, wd)),
    ]
    outs = pl.pallas_call(
        _with_cast_riders(_mlstm_kernel, len(in_specs), 1, len(casts)),
        grid=(batch, nt),
        in_specs=in_specs + cast_in,
        out_specs=[pl.BlockSpec((tcm, wd), lambda b, i: (b * nt + i, 0))] + cast_out,
        out_shape=[jax.ShapeDtypeStruct((n, wd), F32)] + cast_shapes,
        scratch_shapes=[
            pltpu.VMEM((nh, dh, 2 * dh), F32),
            pltpu.VMEM((1, LANES), F32),
        ],
        compiler_params=pltpu.CompilerParams(
            dimension_semantics=("parallel", "arbitrary"), vmem_limit_bytes=VMEM_LIMIT),
        name="mlstm",
    )(qk, p_m, gate_bi, gate_bf, ln_g, *[c[0] for c in casts])
    return outs[0], list(outs[1:])


def _mixout_ffn_kernel(x_ref, yr_ref, ym_ref, pg_ref, mod_ref, wr_ref, wm_ref, wo_ref,
                       g_ref, wi_ref, wf_ref, fg_ref, o_ref, *, final_norm):
    dm = x_ref.shape[1]
    zrs = [jnp.dot(y.astype(BF16), wr_ref[...], preferred_element_type=F32) for y in _row_parts(yr_ref, ROW_SPLIT)]
    zms = [jnp.dot(y.astype(BF16), wm_ref[...], preferred_element_type=F32) for y in _row_parts(ym_ref, ROW_SPLIT)]
    mixes = [(jax.nn.sigmoid(pg[:, 0:dm]) * zr + jax.nn.sigmoid(pg[:, dm:2 * dm]) * zm).astype(BF16)
             for pg, zr, zm in zip(_row_parts(pg_ref, ROW_SPLIT), zrs, zms)]
    outs = [jnp.dot(m, wo_ref[...], preferred_element_type=F32) for m in mixes]
    xs = [x + (1.0 + mod_ref[0, 5:6, :]) * o for x, o in zip(_row_parts(x_ref, ROW_SPLIT), outs)]
    _store_row_parts(o_ref, _ffn_half_step(xs, mod_ref, g_ref, wi_ref, wf_ref, fg_ref, mod_row=6,
                                           final_norm=final_norm))


def _mixout_ffn_call(x2, y_r, y_m, p_g, mod_l, w_r, w_m, w_o, norm_g, w_in, w_out, final_g, *,
                     layer, seq, final_norm, tm):
    n, dm = x2.shape
    rowblk = lambda cols: pl.BlockSpec((tm, cols), lambda i: (i, 0))
    kern = functools.partial(_mixout_ffn_kernel, final_norm=final_norm)
    return pl.pallas_call(
        kern,
        grid=(n // tm,),
        in_specs=[
            rowblk(dm), rowblk(y_r.shape[1]), rowblk(y_m.shape[1]), rowblk(p_g.shape[1]),
            pl.BlockSpec((1, 9, dm), lambda i: ((i * tm) // seq, 0, 0)),
            _resident_at(w_r, (layer,)), _resident_at(w_m, (layer,)), _resident_at(w_o, (layer,)),
            _resident((1, dm)), _resident_at(w_in, ()), _resident_at(w_out, ()),
            _resident((1, dm)),
        ],
        out_specs=rowblk(dm),
        out_shape=jax.ShapeDtypeStruct((n, dm), F32),
        compiler_params=pltpu.CompilerParams(
            dimension_semantics=("parallel",), vmem_limit_bytes=VMEM_LIMIT),
        name="mix_out_ffn",
    )(x2, y_r, y_m, p_g, mod_l, w_r, w_m, w_o, norm_g, w_in, w_out, final_g)


def _pad_cols(a, n):
    return jnp.pad(a, ((0, 0), (0, n - a.shape[1])))


def _row(a, n=None):
    a = a.reshape(1, -1)
    return a if n is None else _pad_cols(a, n)


def _lowrank_pad(mat, row0):
    out = jnp.zeros((RWKV_LOWRANK_PAD, mat.shape[1]), F32)
    return lax.dynamic_update_slice(out, mat, (row0, 0)).astype(BF16)


def kernel(x, c, ada_w, ada_b, norm_g, ffn_w_in, ffn_w_out, mix_w_in, rwkv_mu, rwkv_w0, rwkv_w_up, rwkv_a0, rwkv_a_up, rwkv_g_up, rwkv_k_k, rwkv_k_a, rwkv_r_k, rwkv_ln_g, rwkv_ln_b, mlstm_conv_w, mlstm_conv_b, mlstm_gate_b, mlstm_ln_g, branch_w_rwkv, branch_w_mlstm, mix_w_out, final_g):
    batch, seq, dm = x.shape
    depth = ada_w.shape[0]
    n = batch * seq
    rw = RWKV_HEADS * RWKV_HEAD_DIM
    mw = MLSTM_HEADS * MLSTM_HEAD_DIM
    rwkv_cols = 3 * rw + RWKV_DECAY_RANK + RWKV_A_RANK + RWKV_GATE_RANK
    mlstm_cols = 4 * mw + 2 * MLSTM_HEADS
    rwkv_pad = 3 * rw + RWKV_LOWRANK_PAD

    tm_ffn = min(512, seq)
    tm_mix = min(512, seq)
    tc_rwkv = min(512, seq)
    tc_mlstm = min(512, seq)

    c_pad = jnp.pad(c, ((0, SUBLANES - batch % SUBLANES if batch % SUBLANES else 0), (0, 0)))
    mod = _ada_call(c_pad, ada_w, ada_b)[:, :batch].reshape(depth, batch, 9, dm)

    v0 = rwkv_cols + 2 * mw
    gate0 = rwkv_cols + 4 * mw
    lane_pad = lambda a, w: jnp.pad(a, ((0, 0), (0, 0), (0, w - a.shape[-1])))
    mix_w_all = jnp.concatenate([
        mix_w_in[:, :, rwkv_cols:v0],
        lane_pad(mix_w_in[:, :, :rwkv_cols], rwkv_pad),
        mix_w_in[:, :, v0:gate0],
        lane_pad(mix_w_in[:, :, gate0:gate0 + MLSTM_HEADS], LANES),
        lane_pad(mix_w_in[:, :, gate0 + MLSTM_HEADS:gate0 + 2 * MLSTM_HEADS], LANES),
        mix_w_in[:, :, rwkv_cols + mlstm_cols:]], axis=-1).astype(BF16)
    mix_widths = [rwkv_pad, 2 * mw, 2 * mw + 2 * LANES, 2 * dm]

    branch_r_bf = branch_w_rwkv.astype(BF16)
    branch_m_bf = branch_w_mlstm.astype(BF16)
    mix_w_out_bf = mix_w_out.astype(BF16)
    final_row = _row(final_g)

    ffn_in_bf = {(0, 0): ffn_w_in[0, 0].astype(BF16)}
    ffn_out_bf = {(0, 0): ffn_w_out[0, 0].astype(BF16)}

    x2 = x.reshape(n, dm)
    for l in range(depth):
        mod_l = mod[l]
        x2 = _ffn_call(x2, mod_l, _row(norm_g[l, 0]), ffn_in_bf[(l, 0)], ffn_out_bf[(l, 0)], final_row,
                       lead=(), seq=seq, mod_row=0, final_norm=False, tm=tm_ffn)
        later = [(l, 1)] + ([(l + 1, 0)] if l + 1 < depth else [])

        xs_r, qk_m, p_m, p_g = _mixin_call(
            x2, mod_l, _row(norm_g[l, 1]), mix_w_all, mix_widths, _row(rwkv_mu[l], rwkv_pad),
            mlstm_conv_w[l], _row(mlstm_conv_b[l]), layer=l, seq=seq, tm=tm_mix)

        y_r, cast_in = _rwkv_call(
            xs_r, _row(rwkv_w0[l]),
            _lowrank_pad(rwkv_w_up[l], 0), _row(rwkv_a0[l]),
            _lowrank_pad(rwkv_a_up[l], RWKV_DECAY_RANK),
            _lowrank_pad(rwkv_g_up[l], RWKV_DECAY_RANK + RWKV_A_RANK),
            _row(rwkv_k_k[l]), _row(rwkv_k_a[l]), _row(rwkv_r_k[l]), _row(rwkv_ln_g[l]), _row(rwkv_ln_b[l]),
            batch=batch, seq=seq, tc=tc_rwkv, casts=[(ffn_w_in, lf) for lf in later])
        y_m, cast_out = _mlstm_call(
            qk_m, p_m, _row(mlstm_gate_b[l, :MLSTM_HEADS], LANES), _row(mlstm_gate_b[l, MLSTM_HEADS:], LANES),
            _row(mlstm_ln_g[l]), batch=batch, seq=seq, tcm=tc_mlstm, casts=[(ffn_w_out, lf) for lf in later])
        ffn_in_bf.update(zip(later, cast_in))
        ffn_out_bf.update(zip(later, cast_out))

        x2 = _mixout_ffn_call(
            x2, y_r, y_m, p_g, mod_l, branch_r_bf, branch_m_bf, mix_w_out_bf, _row(norm_g[l, 2]),
            ffn_in_bf[(l, 1)], ffn_out_bf[(l, 1)], final_row, layer=l, seq=seq, final_norm=(l == depth - 1),
            tm=tm_ffn)
    return x2.reshape(batch, seq, dm)
```

```python
import functools
import math

import jax
import jax.numpy as jnp
from jax import lax
from jax.experimental import pallas as pl
from jax.experimental.pallas import tpu as pltpu

F32 = jnp.float32
BF16 = jnp.bfloat16

RWKV_HEADS = 8
RWKV_HEAD_DIM = 64
RWKV_DECAY_RANK = 32
RWKV_A_RANK = 32
RWKV_GATE_RANK = 96
MLSTM_HEADS = 4
MLSTM_HEAD_DIM = 128
MLSTM_CONV = 4
MLSTM_CHUNK = 128
RMS_EPS = 1e-6
RWKV_GN_EPS = 64e-5
MLSTM_NORM_EPS = 1e-5
L2_EPS = 1e-12

LANES = 128
SUBLANES = 8
VMEM_LIMIT = 56 * 1024 * 1024
ROW_SPLIT = 2

RWKV_CHUNK = 64
RWKV_GROUP_HEADS = 2
RWKV_PREP_UNROLL = 4
RWKV_LOWRANK_PAD = 256


def _bdot(a, b):
    return jnp.dot(a.astype(BF16), b.astype(BF16), preferred_element_type=F32)


def _bdot_nt(a, b):
    return lax.dot_general(a.astype(BF16), b.astype(BF16), (((1,), (1,)), ((), ())),
                           preferred_element_type=F32)


def _bdot_tn(a, b):
    return lax.dot_general(a.astype(BF16), b.astype(BF16), (((0,), (0,)), ((), ())),
                           preferred_element_type=F32)


def _split_bf16(x, parts):
    out = []
    rem = x
    for _ in range(parts):
        p = rem.astype(BF16)
        out.append(p)
        rem = rem - p.astype(F32)
    return out


def _dot_exact_rhs(x, m_bf16, parts):
    acc = None
    for p in _split_bf16(x, parts):
        t = jnp.dot(p, m_bf16, preferred_element_type=F32)
        acc = t if acc is None else acc + t
    return acc


def _exact_lhs_dot(m_bf16, x, parts):
    acc = None
    for p in _split_bf16(x, parts):
        t = jnp.dot(m_bf16, p, preferred_element_type=F32)
        acc = t if acc is None else acc + t
    return acc


def _rms_mod(x, g, shift, scale):
    y = x * lax.rsqrt(jnp.mean(x * x, axis=-1, keepdims=True) + RMS_EPS) * g
    return y * (1.0 + scale) + shift


def _softplus(z):
    return jnp.maximum(z, 0.0) + jnp.log1p(jnp.exp(-jnp.abs(z)))


def _ada_kernel(c_ref, w_ref, b_ref, o_ref):
    c = c_ref[...]
    cond = c * jax.nn.sigmoid(c)
    o_ref[0] = _bdot(cond, w_ref[0]) + b_ref[0]


def _ada_call(c_pad, ada_w, ada_b):
    depth, dm, n9 = ada_w.shape
    tn = n9 // 4
    rows = c_pad.shape[0]
    return pl.pallas_call(
        _ada_kernel,
        grid=(depth, n9 // tn),
        in_specs=[
            pl.BlockSpec((rows, dm), lambda l, j: (0, 0)),
            pl.BlockSpec((1, dm, tn), lambda l, j: (l, 0, j)),
            pl.BlockSpec((1, 1, tn), lambda l, j: (l, 0, j)),
        ],
        out_specs=pl.BlockSpec((1, rows, tn), lambda l, j: (l, 0, j)),
        out_shape=jax.ShapeDtypeStruct((depth, rows, n9), F32),
        compiler_params=pltpu.CompilerParams(
            dimension_semantics=("arbitrary", "arbitrary"), vmem_limit_bytes=VMEM_LIMIT),
        name="ada_mod",
    )(c_pad, ada_w, ada_b.reshape(depth, 1, n9))


def _row_parts(ref, parts):
    rows = ref.shape[0] // parts
    return [ref[k * rows:(k + 1) * rows, :] for k in range(parts)]


def _store_row_parts(ref, vals):
    rows = ref.shape[0] // len(vals)
    for k, v in enumerate(vals):
        ref[k * rows:(k + 1) * rows, :] = v


def _ffn_half_step(xs, mod_ref, g_ref, wi_ref, wo_ref, fg_ref, *, mod_row, final_norm):
    dff = wo_ref.shape[0]
    shift, scale = mod_ref[0, mod_row:mod_row + 1, :], mod_ref[0, mod_row + 1:mod_row + 2, :]
    hs = [_rms_mod(x, g_ref[...], shift, scale).astype(BF16) for x in xs]
    gus = [jnp.dot(h, wi_ref[...], preferred_element_type=F32) for h in hs]
    acts = [((gu[:, :dff] * jax.nn.sigmoid(gu[:, :dff])) * gu[:, dff:]).astype(BF16) for gu in gus]
    ys = [jnp.dot(a, wo_ref[...], preferred_element_type=F32) for a in acts]
    outs = [x + (0.5 * (1.0 + mod_ref[0, mod_row + 2:mod_row + 3, :])) * y for x, y in zip(xs, ys)]
    if final_norm:
        outs = [o * lax.rsqrt(jnp.mean(o * o, axis=-1, keepdims=True) + RMS_EPS) * fg_ref[...] for o in outs]
    return outs


def _ffn_kernel(x_ref, mod_ref, g_ref, wi_ref, wo_ref, fg_ref, o_ref, *, mod_row, final_norm):
    outs = _ffn_half_step(_row_parts(x_ref, ROW_SPLIT), mod_ref, g_ref, wi_ref, wo_ref, fg_ref,
                          mod_row=mod_row, final_norm=final_norm)
    _store_row_parts(o_ref, outs)


def _resident(shape):
    return pl.BlockSpec(shape, lambda *_: (0,) * len(shape), pipeline_mode=pl.Buffered(1))


def _resident_at(arr, lead):
    tail = arr.shape[len(lead):]
    return pl.BlockSpec((None,) * len(lead) + tail, lambda *_: tuple(lead) + (0,) * len(tail),
                        pipeline_mode=pl.Buffered(1))


def _ffn_call(x2, mod_l, norm_g, w_in, w_out, final_g, *, lead, seq, mod_row, final_norm, tm):
    n, dm = x2.shape
    kern = functools.partial(_ffn_kernel, mod_row=mod_row, final_norm=final_norm)
    return pl.pallas_call(
        kern,
        grid=(n // tm,),
        in_specs=[
            pl.BlockSpec((tm, dm), lambda i: (i, 0)),
            pl.BlockSpec((1, 9, dm), lambda i: ((i * tm) // seq, 0, 0)),
            _resident((1, dm)),
            _resident_at(w_in, lead),
            _resident_at(w_out, lead),
            _resident((1, dm)),
        ],
        out_specs=pl.BlockSpec((tm, dm), lambda i: (i, 0)),
        out_shape=jax.ShapeDtypeStruct((n, dm), F32),
        compiler_params=pltpu.CompilerParams(
            dimension_semantics=("parallel",), vmem_limit_bytes=VMEM_LIMIT),
        name="ffn",
    )(x2, mod_l, norm_g, w_in, w_out, final_g)


def _mixin_kernel(x_ref, mod_ref, g_ref, w_ref, mu_ref, cw_ref, cb_ref,
                  xs_ref, qk_ref, pm_ref, pg_ref, prev_ref, xbuf, *, tiles_per_seq):
    tm = x_ref.shape[0]
    KC = cw_ref.shape[0]
    first = (pl.program_id(0) % tiles_per_seq) == 0
    c1 = qk_ref.shape[1]
    c2 = c1 + xs_ref.shape[1]
    c3 = c2 + pm_ref.shape[1]
    c4 = c3 + pg_ref.shape[1]

    @pl.when(first)
    def _():
        prev_ref[...] = jnp.zeros_like(prev_ref)
        xbuf[0:SUBLANES, :] = jnp.zeros((SUBLANES, xbuf.shape[1]), F32)

    @pl.when(jnp.logical_not(first))
    def _():
        xbuf[0:SUBLANES, :] = xbuf[tm:tm + SUBLANES, :]

    h = _rms_mod(x_ref[...], g_ref[...], mod_ref[0, 3:4, :], mod_ref[0, 4:5, :]).astype(BF16)

    pr = jnp.dot(h, w_ref[:, c1:c2], preferred_element_type=F32)
    row = lax.broadcasted_iota(jnp.int32, (tm, 1), 0)
    prev = jnp.where(row == 0, prev_ref[...], pltpu.roll(pr, 1, axis=0))
    prev_ref[...] = pr[tm - 1:tm, :]
    xs_ref[...] = pr + (prev - pr) * mu_ref[...]

    xbuf[SUBLANES:SUBLANES + tm, :] = jnp.dot(h, w_ref[:, 0:c1], preferred_element_type=F32)
    acc = cb_ref[...] + cw_ref[KC - 1:KC, :] * xbuf[SUBLANES:SUBLANES + tm, :]
    for j in range(KC - 1):
        off = SUBLANES - (KC - 1) + j
        acc = acc + cw_ref[j:j + 1, :] * xbuf[off:off + tm, :]
    qk_ref[...] = acc * jax.nn.sigmoid(acc)

    pm_ref[...] = jnp.dot(h, w_ref[:, c2:c3], preferred_element_type=F32)
    pg_ref[...] = jnp.dot(h, w_ref[:, c3:c4], preferred_element_type=F32)


def _mixin_call(x2, mod_l, norm_g, w_all, widths, mu, conv_w, conv_b, *, layer, seq, tm):
    n, dm = x2.shape
    assert seq % tm == 0 and conv_w.shape[0] - 1 <= SUBLANES and sum(widths) == w_all.shape[-1]
    kern = functools.partial(_mixin_kernel, tiles_per_seq=seq // tm)
    return pl.pallas_call(
        kern,
        grid=(n // tm,),
        in_specs=[
            pl.BlockSpec((tm, dm), lambda i: (i, 0)),
            pl.BlockSpec((1, 9, dm), lambda i: ((i * tm) // seq, 0, 0)),
            _resident((1, dm)),
            _resident_at(w_all, (layer,)),
            _resident(mu.shape), _resident(conv_w.shape), _resident(conv_b.shape),
        ],
        out_specs=[pl.BlockSpec((tm, w), lambda i: (i, 0)) for w in widths],
        out_shape=[jax.ShapeDtypeStruct((n, w), F32) for w in widths],
        scratch_shapes=[pltpu.VMEM((1, widths[0]), F32), pltpu.VMEM((tm + SUBLANES, widths[1]), F32)],
        compiler_params=pltpu.CompilerParams(
            dimension_semantics=("arbitrary",), vmem_limit_bytes=VMEM_LIMIT),
        name="mix_in",
    )(x2, mod_l, norm_g, w_all, mu, conv_w, conv_b)


def _rwkv_kernel(xs_ref, w0_ref, wup_ref, a0_ref, aup_ref, gup_ref, kk_ref, ka_ref, rk_ref,
                 lng_ref, lnb_ref, o_ref,
                 s_ref, r_s, y_s, bonus_s, g_s, mx_s, n0_s, pc_s, side_work=iter(())):
    tc = xs_ref.shape[0]
    width = o_ref.shape[1]
    C = RWKV_CHUNK
    HD = RWKV_HEAD_DIM
    HG = RWKV_GROUP_HEADS
    GL = HG * HD
    n_groups = width // GL
    shift = HD.bit_length() - 1
    i = pl.program_id(1)

    @pl.when(i == 0)
    def _():
        s_ref[...] = jnp.zeros_like(s_ref)

    rg = lax.broadcasted_iota(jnp.int32, (GL, GL), 0)
    cg = lax.broadcasted_iota(jnp.int32, (GL, GL), 1)
    blockmask = (rg >> shift) == (cg >> shift)
    eye = rg == cg
    ones_bd = jnp.where(blockmask, 1.0, 0.0).astype(BF16)

    def segsum(x):
        return jnp.concatenate(
            [_dot_exact_rhs(x[:, g * GL:(g + 1) * GL], ones_bd, 2) for g in range(n_groups)], axis=1)

    def token_quantities(rows):
        r = xs_ref[rows, 0:width]
        k = xs_ref[rows, width:2 * width]
        v = xs_ref[rows, 2 * width:3 * width]
        low = xs_ref[rows, 3 * width:3 * width + RWKV_LOWRANK_PAD]
        w = w0_ref[...] + _bdot(jnp.tanh(low), wup_ref[...])
        ld = (-math.exp(-0.5)) * jax.nn.sigmoid(w)
        asig = jax.nn.sigmoid(a0_ref[...] + _bdot(low, aup_ref[...]))
        g_s[rows, :] = _bdot(jax.nn.sigmoid(low), gup_ref[...])
        kk = k * kk_ref[...]
        kk = kk * lax.rsqrt(jnp.maximum(segsum(kk * kk), L2_EPS * L2_EPS))
        k2 = k * (1.0 + (asig - 1.0) * ka_ref[...])
        bonus_s[rows, :] = segsum(r * k2 * rk_ref[...]) * v
        return dict(r=r, ld=ld, k=k2, v=v, a=-kk, b=kk * asig)

    lane = lax.broadcasted_iota(jnp.int32, (C, GL), 1)
    tok = lax.broadcasted_iota(jnp.int32, (C, GL), 0)
    strict = (lane & (HD - 1)) < tok
    incl = (lane & (HD - 1)) <= tok
    headmask = [(lane >> shift) == h for h in range(HG)]
    tri = (lax.broadcasted_iota(jnp.int32, (C, C), 1) <= lax.broadcasted_iota(jnp.int32, (C, C), 0))
    tri = jnp.where(tri, 1.0, 0.0).astype(BF16)

    def bd(x):
        return jnp.where(blockmask, jnp.concatenate([x] * HG, axis=0), 0.0).astype(BF16)

    levels = C.bit_length() - 1

    def prepare_stages(c_first):
        inst = []
        tq = token_quantities(slice(c_first * C, (c_first + RWKV_PREP_UNROLL) * C))
        yield
        for u in range(RWKV_PREP_UNROLL):
            c = c_first + u
            sl = slice(c * C, (c + 1) * C)
            loc = slice(u * C, (u + 1) * C)
            ld = tq['ld'][loc]
            lp = _exact_lhs_dot(tri, ld, 3)
            lpc = lp[C - 1:C, :]
            e_lp = jnp.exp(lp)
            e_nlp = jnp.exp(-lp)
            e_end = jnp.exp(lpc - lp)
            rt = tq['r'][loc] * e_lp
            at = tq['a'][loc] * jnp.exp(lp - ld)
            bb = tq['b'][loc]
            kc = tq['k'][loc]
            bt = bb * e_nlp
            kt = kc * e_nlp
            bh = bb * e_end
            kh = kc * e_end
            vv = tq['v'][loc]
            pc = jnp.exp(lpc)
            for g in range(n_groups):
                gs = slice(g * GL, (g + 1) * GL)
                inst.append(dict(sl=sl, gs=gs, idx=c * n_groups + g, at=at[:, gs], rt=rt[:, gs], v=vv[:, gs],
                                 bt=bt[:, gs], kt=kt[:, gs], bh=bh[:, gs], kh=kh[:, gs], pc=pc[:, gs]))
        for t in inst:
            lhs = jnp.concatenate([t['at'], t['rt']], axis=0)
            rows = [jnp.where(headmask[h], t['bt'], 0.0) for h in range(HG)]
            rows += [jnp.where(headmask[h], t['kt'], 0.0) for h in range(HG)]
            gram = _bdot_nt(lhs, jnp.concatenate(rows, axis=0))
            t['a_ab'] = jnp.where(strict, gram[:C, :GL], 0.0)
            t['a_ak'] = jnp.where(strict, gram[:C, GL:], 0.0)
            t['a_rb'] = jnp.where(incl, gram[C:, :GL], 0.0).astype(BF16)
            t['a_rk'] = jnp.where(incl, gram[C:, GL:], 0.0)
        yield
        for t in inst:
            t['bdv'] = bd(t['v'])
            both = _bdot(jnp.concatenate([t['a_ak'], t['a_rk']], axis=0), t['bdv'])
            t['akv'] = both[:C]
            t['arkv'] = both[C:]
        yield
        for t in inst:
            t['tr'] = t['a_ab']
            t['nm'] = jnp.dot(t['a_ab'].astype(BF16), bd(t['a_ab']), preferred_element_type=F32)
        yield
        for lvl in range(1, levels):
            for t in inst:
                if lvl + 1 < levels:
                    both = _bdot(jnp.concatenate([t['tr'], t['nm']], axis=0), bd(t['nm']))
                    t['tr'] = t['tr'] + t['nm'] + both[:C]
                    t['nm'] = both[C:]
                else:
                    t['tr'] = t['tr'] + t['nm'] + _bdot(t['tr'], bd(t['nm']))
            yield
        for t in inst:
            wu = _bdot(t['tr'], jnp.concatenate([bd(t['at']), bd(t['akv'])], axis=1))
            t['w'] = t['at'] + wu[:, :GL]
            t['u0'] = t['akv'] + wu[:, GL:]
        yield
        for t in inst:
            ry = jnp.dot(t['a_rb'], jnp.concatenate([bd(t['w']), bd(t['u0'])], axis=1),
                         preferred_element_type=F32)
            r_s[t['sl'], t['gs']] = t['rt'] + ry[:, :GL]
            y_s[t['sl'], t['gs']] = ry[:, GL:] + t['arkv']
        yield
        for t in inst:
            mn = _bdot_tn(t['bh'], jnp.concatenate([t['w'], t['u0']], axis=1))
            mx_s[t['idx']] = jnp.where(blockmask, mn[:, :GL], 0.0).astype(BF16)
            n0_s[t['idx']] = jnp.where(blockmask, mn[:, GL:] + _bdot_tn(t['kh'], t['v']), 0.0)
            pc_s[t['idx']] = jnp.broadcast_to(
                jnp.sum(jnp.where(eye, jnp.broadcast_to(t['pc'], (GL, GL)), 0.0), axis=1, keepdims=True),
                (GL, GL))

    def advance(c):
        sl = slice(c * C, (c + 1) * C)
        for g in range(n_groups):
            gs = slice(g * GL, (g + 1) * GL)
            idx = c * n_groups + g
            s_old = s_ref[g]
            s_bf = s_old.astype(BF16)
            y_s[sl, gs] = y_s[sl, gs] + jnp.dot(r_s[sl, gs].astype(BF16), s_bf, preferred_element_type=F32)
            s_ref[g] = pc_s[idx] * s_old + n0_s[idx] + jnp.dot(mx_s[idx], s_bf, preferred_element_type=F32)

    def finish(rows):
        y = y_s[rows, :]
        inv = 1.0 / HD
        yc = y - segsum(y) * inv
        var = segsum(yc * yc) * inv
        yn = yc * lax.rsqrt(var + RWKV_GN_EPS) * lng_ref[...] + lnb_ref[...]
        o_ref[rows, :] = (yn + bonus_s[rows, :]) * g_s[rows, :]

    n_chunks = tc // C
    pending = []
    done_rows = None
    for c_first in range(0, n_chunks, RWKV_PREP_UNROLL):
        stages = prepare_stages(c_first)
        for k, _ in enumerate(stages):
            if pending and k % 2 == 0:
                advance(pending.pop(0))
            elif not pending and done_rows is not None:
                finish(done_rows)
                done_rows = None
            next(side_work, None)
        for c in pending:
            advance(c)
        if done_rows is not None:
            finish(done_rows)
        pending = list(range(c_first, c_first + RWKV_PREP_UNROLL))
        done_rows = slice(c_first * C, (c_first + RWKV_PREP_UNROLL) * C)
    for c in pending:
        advance(c)
    finish(done_rows)
    for _ in side_work:
        pass


BF16_SUBLANES = 16


def _cast_blocking(rows, cols, steps):
    for cb in range(1, steps + 1):
        rb = steps // cb
        if (steps % cb == 0 and rows % rb == 0 and cols % cb == 0
                and (rows // rb) % BF16_SUBLANES == 0 and (cols // cb) % LANES == 0):
            return rb, cb
    raise ValueError((rows, cols, steps))


def _cast_rider_specs(casts, nt):
    in_specs, out_specs, out_shapes = [], [], []
    for arr, lead, (rb, cb) in casts:
        rows, cols = arr.shape[-2:]
        blk = (rows // rb, cols // cb)
        in_specs.append(pl.BlockSpec(
            (None,) * len(lead) + blk,
            lambda b, i, lead=lead, cb=cb: tuple(lead) + ((b * nt + i) // cb, (b * nt + i) % cb)))
        out_specs.append(pl.BlockSpec(blk, lambda b, i, cb=cb: ((b * nt + i) // cb, (b * nt + i) % cb)))
        out_shapes.append(jax.ShapeDtypeStruct((rows, cols), BF16))
    return in_specs, out_specs, out_shapes


def _with_cast_riders(body, n_in, n_out, n_cast):
    def kern(*refs):
        ins = refs[:n_in]
        cast_ins = refs[n_in:n_in + n_cast]
        outs = refs[n_in + n_cast:n_in + n_cast + n_out]
        cast_outs = refs[n_in + n_cast + n_out:n_in + 2 * n_cast + n_out]
        scratch = refs[n_in + 2 * n_cast + n_out:]
        for src, dst in zip(cast_ins, cast_outs):
            dst[...] = src[...].astype(BF16)
        body(*ins, *outs, *scratch)
    return kern


def _rwkv_specs(ncols, *, seq, tc):
    width = RWKV_HEADS * RWKV_HEAD_DIM
    gl = RWKV_GROUP_HEADS * RWKV_HEAD_DIM
    nt = seq // tc
    assert seq % tc == 0 and tc % (RWKV_CHUNK * RWKV_PREP_UNROLL) == 0 and width % gl == 0
    nblk = (tc // RWKV_CHUNK) * (width // gl)
    vec = _resident((1, width))
    lowmat = _resident((RWKV_LOWRANK_PAD, width))
    tile = pltpu.VMEM((tc, width), F32)
    in_specs = [pl.BlockSpec((tc, ncols), lambda b, i: (b * nt + i, 0)),
                vec, lowmat, vec, lowmat, lowmat, vec, vec, vec, vec, vec]
    out_spec = pl.BlockSpec((tc, width), lambda b, i: (b * nt + i, 0))
    scratch = [
        pltpu.VMEM((width // gl, gl, gl), F32),
        tile, tile, tile, tile,
        pltpu.VMEM((nblk, gl, gl), BF16),
        pltpu.VMEM((nblk, gl, gl), F32),
        pltpu.VMEM((nblk, gl, gl), F32),
    ]
    return in_specs, out_spec, scratch


def _cummax_rows(x):
    n = x.shape[0]
    row = lax.broadcasted_iota(jnp.int32, x.shape, 0)
    s = 1
    while s < n:
        x = jnp.maximum(x, jnp.where(row >= s, pltpu.roll(x, s, axis=0), -jnp.inf))
        s *= 2
    return x


def _mlstm_stages(qk_s, p_ref, gbi_ref, gbf_ref, lng_ref, o_ref, cn_ref, m_ref):
    tcm = p_ref.shape[0]
    NH, DH, L = MLSTM_HEADS, MLSTM_HEAD_DIM, MLSTM_CHUNK
    wd = NH * DH
    i = pl.program_id(1)

    @pl.when(i == 0)
    def _():
        cn_ref[...] = jnp.zeros_like(cn_ref)
        m_ref[...] = jnp.full(m_ref.shape, -jnp.inf, F32)

    rr = lax.broadcasted_iota(jnp.int32, (L, L), 0)
    cc = lax.broadcasted_iota(jnp.int32, (L, L), 1)
    causal = cc <= rr
    tri = jnp.where(causal, 1.0, 0.0).astype(BF16)
    eye_l = jnp.where(cc == rr, 1.0, 0.0).astype(BF16)
    eye_h = eye_l[0:2 * SUBLANES, :]
    ones_v = jnp.ones((L, DH), BF16)
    scale = DH ** -0.5
    heads = range(NH)

    def chunk(c):
        sl = slice(c * L, (c + 1) * L)
        li = p_ref[sl, 2 * wd:2 * wd + LANES] + gbi_ref[...]
        lf = -_softplus(-(p_ref[sl, 2 * wd + LANES:2 * wd + 2 * LANES] + gbf_ref[...]))
        b_col = _exact_lhs_dot(tri, lf, 3)
        q = [(qk_s[sl, h * DH:(h + 1) * DH] * scale).astype(BF16) for h in heads]
        k = [qk_s[sl, wd + h * DH:wd + (h + 1) * DH].astype(BF16) for h in heads]
        s = [lax.dot_general(q[h], k[h], (((1,), (1,)), ((), ())), preferred_element_type=F32) for h in heads]
        yield
        xcol = li - b_col
        m_prev = m_ref[...]
        zcol = jnp.maximum(m_prev, _cummax_rows(xcol))
        b_end = b_col[L - 1:L, :]
        m_new = jnp.maximum(b_end + m_prev, jnp.max(b_end + xcol, axis=0, keepdims=True))
        keep = jnp.exp(b_end + m_prev - m_new)
        m_ref[...] = m_new
        wshift = b_end - m_new
        xrow = None
        for part in _split_bf16(xcol, 3):
            t = lax.dot_general(eye_h, part, (((1,), (1,)), ((), ())), preferred_element_type=F32)
            xrow = t if xrow is None else xrow + t
        vx = [jnp.concatenate([p_ref[sl, h * DH:(h + 1) * DH].astype(BF16), ones_v], axis=1)
              for h in heads]
        cn = [cn_ref[h] for h in heads]
        kt = [lax.dot_general(eye_l, k[h], (((1,), (1,)), ((), ())), preferred_element_type=F32) for h in heads]
        qc = [jnp.dot(q[h], cn[h].astype(BF16), preferred_element_type=F32) for h in heads]
        yield
        zb = [jnp.broadcast_to(zcol[:, h:h + 1], (L, DH)) for h in heads]
        bb = [jnp.broadcast_to(b_col[:, h:h + 1], (L, DH)) for h in heads]
        pw = [(s[h] * jnp.exp(jnp.where(causal, xrow[h:h + 1, :] - zb[h], -jnp.inf))).astype(BF16) for h in heads]
        nd = [jnp.dot(pw[h], vx[h], preferred_element_type=F32) for h in heads]
        for h in heads:
            wrow = jnp.exp(xrow[h:h + 1, :] + wshift[:, h:h + 1])
            cn_ref[h] = keep[:, h:h + 1] * cn[h] + jnp.dot((kt[h] * wrow).astype(BF16), vx[h],
                                                            preferred_element_type=F32)
        yield
        for h in heads:
            hs = slice(h * DH, (h + 1) * DH)
            inter = jnp.exp(m_prev[:, h:h + 1] - zb[h])
            num = nd[h][:, :DH] + inter * qc[h][:, :DH]
            den = nd[h][:, DH:] + inter * qc[h][:, DH:]
            hh = num / jnp.maximum(jnp.abs(den), jnp.exp(-(bb[h] + zb[h])))
            hc = hh - jnp.mean(hh, axis=1, keepdims=True)
            hn = hc * lax.rsqrt(jnp.mean(hc * hc, axis=1, keepdims=True) + MLSTM_NORM_EPS)
            og = jax.nn.sigmoid(p_ref[sl, wd + h * DH:wd + (h + 1) * DH])
            o_ref[sl, hs] = og * (hn * lng_ref[:, hs])
        yield

    for c in range(tcm // L):
        yield from chunk(c)


def _mlstm_specs(ncols, *, seq, tc):
    nh, dh = MLSTM_HEADS, MLSTM_HEAD_DIM
    wd = nh * dh
    nt = seq // tc
    assert seq % tc == 0 and tc % MLSTM_CHUNK == 0
    in_specs = [
        pl.BlockSpec((tc, 2 * wd), lambda b, i: (b * nt + i, 0)),
        pl.BlockSpec((tc, ncols), lambda b, i: (b * nt + i, 0)),
        _resident((1, LANES)),
        _resident((1, LANES)),
        _resident((1, wd)),
    ]
    out_spec = pl.BlockSpec((tc, wd), lambda b, i: (b * nt + i, 0))
    scratch = [pltpu.VMEM((nh, dh, 2 * dh), F32), pltpu.VMEM((1, LANES), F32)]
    return in_specs, out_spec, scratch


N_RWKV_IN, N_RWKV_SCRATCH, N_MLSTM_IN = 11, 8, 5


def _mixers_kernel(*refs):
    r_in = refs[:N_RWKV_IN]
    m_in = refs[N_RWKV_IN:N_RWKV_IN + N_MLSTM_IN]
    yr_ref, ym_ref = refs[N_RWKV_IN + N_MLSTM_IN:N_RWKV_IN + N_MLSTM_IN + 2]
    scratch = refs[N_RWKV_IN + N_MLSTM_IN + 2:]
    _rwkv_kernel(*r_in, yr_ref, *scratch[:N_RWKV_SCRATCH],
                 side_work=_mlstm_stages(*m_in, ym_ref, *scratch[N_RWKV_SCRATCH:]))


def _mixers_call(rwkv_args, mlstm_args, *, batch, seq, tc, casts=()):
    n = rwkv_args[0].shape[0]
    nt = seq // tc
    r_in, r_out, r_scratch = _rwkv_specs(rwkv_args[0].shape[1], seq=seq, tc=tc)
    m_in, m_out, m_scratch = _mlstm_specs(mlstm_args[1].shape[1], seq=seq, tc=tc)
    assert (len(r_in), len(r_scratch), len(m_in)) == (N_RWKV_IN, N_RWKV_SCRATCH, N_MLSTM_IN)
    casts = [(arr, lead, _cast_blocking(*arr.shape[-2:], batch * nt)) for arr, lead in casts]
    cast_in, cast_out, cast_shapes = _cast_rider_specs(casts, nt)
    outs = pl.pallas_call(
        _with_cast_riders(_mixers_kernel, len(r_in) + len(m_in), 2, len(casts)),
        grid=(batch, nt),
        in_specs=r_in + m_in + cast_in,
        out_specs=[r_out, m_out] + cast_out,
        out_shape=[jax.ShapeDtypeStruct((n, RWKV_HEADS * RWKV_HEAD_DIM), F32),
                   jax.ShapeDtypeStruct((n, MLSTM_HEADS * MLSTM_HEAD_DIM), F32)] + cast_shapes,
        scratch_shapes=r_scratch + m_scratch,
        compiler_params=pltpu.CompilerParams(
            dimension_semantics=("parallel", "arbitrary"), vmem_limit_bytes=VMEM_LIMIT),
        name="mixers",
    )(*rwkv_args, *mlstm_args, *[c[0] for c in casts])
    return outs[0], outs[1], list(outs[2:])


def _mixout_ffn_kernel(x_ref, yr_ref, ym_ref, pg_ref, mod_ref, wr_ref, wm_ref, wo_ref,
                       g_ref, wi_ref, wf_ref, fg_ref, o_ref, *, final_norm):
    dm = x_ref.shape[1]
    zrs = [jnp.dot(y.astype(BF16), wr_ref[...], preferred_element_type=F32) for y in _row_parts(yr_ref, ROW_SPLIT)]
    zms = [jnp.dot(y.astype(BF16), wm_ref[...], preferred_element_type=F32) for y in _row_parts(ym_ref, ROW_SPLIT)]
    mixes = [(jax.nn.sigmoid(pg[:, 0:dm]) * zr + jax.nn.sigmoid(pg[:, dm:2 * dm]) * zm).astype(BF16)
             for pg, zr, zm in zip(_row_parts(pg_ref, ROW_SPLIT), zrs, zms)]
    outs = [jnp.dot(m, wo_ref[...], preferred_element_type=F32) for m in mixes]
    xs = [x + (1.0 + mod_ref[0, 5:6, :]) * o for x, o in zip(_row_parts(x_ref, ROW_SPLIT), outs)]
    _store_row_parts(o_ref, _ffn_half_step(xs, mod_ref, g_ref, wi_ref, wf_ref, fg_ref, mod_row=6,
                                           final_norm=final_norm))


def _mixout_ffn_call(x2, y_r, y_m, p_g, mod_l, w_r, w_m, w_o, norm_g, w_in, w_out, final_g, *,
                     layer, seq, final_norm, tm):
    n, dm = x2.shape
    rowblk = lambda cols: pl.BlockSpec((tm, cols), lambda i: (i, 0))
    kern = functools.partial(_mixout_ffn_kernel, final_norm=final_norm)
    return pl.pallas_call(
        kern,
        grid=(n // tm,),
        in_specs=[
            rowblk(dm), rowblk(y_r.shape[1]), rowblk(y_m.shape[1]), rowblk(p_g.shape[1]),
            pl.BlockSpec((1, 9, dm), lambda i: ((i * tm) // seq, 0, 0)),
            _resident_at(w_r, (layer,)), _resident_at(w_m, (layer,)), _resident_at(w_o, (layer,)),
            _resident((1, dm)), _resident_at(w_in, ()), _resident_at(w_out, ()),
            _resident((1, dm)),
        ],
        out_specs=rowblk(dm),
        out_shape=jax.ShapeDtypeStruct((n, dm), F32),
        compiler_params=pltpu.CompilerParams(
            dimension_semantics=("parallel",), vmem_limit_bytes=VMEM_LIMIT),
        name="mix_out_ffn",
    )(x2, y_r, y_m, p_g, mod_l, w_r, w_m, w_o, norm_g, w_in, w_out, final_g)


def _pad_cols(a, n):
    return jnp.pad(a, ((0, 0), (0, n - a.shape[1])))


def _row(a, n=None):
    a = a.reshape(1, -1)
    return a if n is None else _pad_cols(a, n)


def _lowrank_pad(mat, row0):
    out = jnp.zeros((RWKV_LOWRANK_PAD, mat.shape[1]), F32)
    return lax.dynamic_update_slice(out, mat, (row0, 0)).astype(BF16)


def kernel(x, c, ada_w, ada_b, norm_g, ffn_w_in, ffn_w_out, mix_w_in, rwkv_mu, rwkv_w0, rwkv_w_up, rwkv_a0, rwkv_a_up, rwkv_g_up, rwkv_k_k, rwkv_k_a, rwkv_r_k, rwkv_ln_g, rwkv_ln_b, mlstm_conv_w, mlstm_conv_b, mlstm_gate_b, mlstm_ln_g, branch_w_rwkv, branch_w_mlstm, mix_w_out, final_g):
    batch, seq, dm = x.shape
    depth = ada_w.shape[0]
    n = batch * seq
    rw = RWKV_HEADS * RWKV_HEAD_DIM
    mw = MLSTM_HEADS * MLSTM_HEAD_DIM
    rwkv_cols = 3 * rw + RWKV_DECAY_RANK + RWKV_A_RANK + RWKV_GATE_RANK
    mlstm_cols = 4 * mw + 2 * MLSTM_HEADS
    rwkv_pad = 3 * rw + RWKV_LOWRANK_PAD

    tm_ffn = min(512, seq)
    tm_mix = min(512, seq)
    tc_mixers = min(512, seq)

    c_pad = jnp.pad(c, ((0, SUBLANES - batch % SUBLANES if batch % SUBLANES else 0), (0, 0)))
    mod = _ada_call(c_pad, ada_w, ada_b)[:, :batch].reshape(depth, batch, 9, dm)

    v0 = rwkv_cols + 2 * mw
    gate0 = rwkv_cols + 4 * mw
    lane_pad = lambda a, w: jnp.pad(a, ((0, 0), (0, 0), (0, w - a.shape[-1])))
    mix_w_all = jnp.concatenate([
        mix_w_in[:, :, rwkv_cols:v0],
        lane_pad(mix_w_in[:, :, :rwkv_cols], rwkv_pad),
        mix_w_in[:, :, v0:gate0],
        lane_pad(mix_w_in[:, :, gate0:gate0 + MLSTM_HEADS], LANES),
        lane_pad(mix_w_in[:, :, gate0 + MLSTM_HEADS:gate0 + 2 * MLSTM_HEADS], LANES),
        mix_w_in[:, :, rwkv_cols + mlstm_cols:]], axis=-1).astype(BF16)
    mix_widths = [rwkv_pad, 2 * mw, 2 * mw + 2 * LANES, 2 * dm]

    branch_r_bf = branch_w_rwkv.astype(BF16)
    branch_m_bf = branch_w_mlstm.astype(BF16)
    mix_w_out_bf = mix_w_out.astype(BF16)
    final_row = _row(final_g)

    ffn_in_bf = {(0, 0): ffn_w_in[0, 0].astype(BF16)}
    ffn_out_bf = {(0, 0): ffn_w_out[0, 0].astype(BF16)}

    x2 = x.reshape(n, dm)
    for l in range(depth):
        mod_l = mod[l]
        x2 = _ffn_call(x2, mod_l, _row(norm_g[l, 0]), ffn_in_bf[(l, 0)], ffn_out_bf[(l, 0)], final_row,
                       lead=(), seq=seq, mod_row=0, final_norm=False, tm=tm_ffn)
        later = [(l, 1)] + ([(l + 1, 0)] if l + 1 < depth else [])

        xs_r, qk_m, p_m, p_g = _mixin_call(
            x2, mod_l, _row(norm_g[l, 1]), mix_w_all, mix_widths, _row(rwkv_mu[l], rwkv_pad),
            mlstm_conv_w[l], _row(mlstm_conv_b[l]), layer=l, seq=seq, tm=tm_mix)

        rwkv_args = [
            xs_r, _row(rwkv_w0[l]),
            _lowrank_pad(rwkv_w_up[l], 0), _row(rwkv_a0[l]),
            _lowrank_pad(rwkv_a_up[l], RWKV_DECAY_RANK),
            _lowrank_pad(rwkv_g_up[l], RWKV_DECAY_RANK + RWKV_A_RANK),
            _row(rwkv_k_k[l]), _row(rwkv_k_a[l]), _row(rwkv_r_k[l]), _row(rwkv_ln_g[l]), _row(rwkv_ln_b[l])]
        mlstm_args = [
            qk_m, p_m, _row(mlstm_gate_b[l, :MLSTM_HEADS], LANES), _row(mlstm_gate_b[l, MLSTM_HEADS:], LANES),
            _row(mlstm_ln_g[l])]
        y_r, y_m, cast = _mixers_call(
            rwkv_args, mlstm_args, batch=batch, seq=seq, tc=tc_mixers,
            casts=[(ffn_w_in, lf) for lf in later] + [(ffn_w_out, lf) for lf in later])
        ffn_in_bf.update(zip(later, cast[:len(later)]))
        ffn_out_bf.update(zip(later, cast[len(later):]))

        x2 = _mixout_ffn_call(
            x2, y_r, y_m, p_g, mod_l, branch_r_bf, branch_m_bf, mix_w_out_bf, _row(norm_g[l, 2]),
            ffn_in_bf[(l, 1)], ffn_out_bf[(l, 1)], final_row, layer=l, seq=seq, final_norm=(l == depth - 1),
            tm=tm_ffn)
    return x2.reshape(batch, seq, dm)
```

```python
import functools
import math

import jax
import jax.numpy as jnp
from jax import lax
from jax.experimental import pallas as pl
from jax.experimental.pallas import tpu as pltpu

F32 = jnp.float32
BF16 = jnp.bfloat16

RWKV_HEADS = 8
RWKV_HEAD_DIM = 64
RWKV_DECAY_RANK = 32
RWKV_A_RANK = 32
RWKV_GATE_RANK = 96
MLSTM_HEADS = 4
MLSTM_HEAD_DIM = 128
MLSTM_CONV = 4
MLSTM_CHUNK = 128
RMS_EPS = 1e-6
RWKV_GN_EPS = 64e-5
MLSTM_NORM_EPS = 1e-5
L2_EPS = 1e-12

LANES = 128
SUBLANES = 8
VMEM_LIMIT = 56 * 1024 * 1024
ROW_SPLIT = 2

RWKV_CHUNK = 64
RWKV_GROUP_HEADS = 2
RWKV_PREP_UNROLL = 4
RWKV_LOWRANK_PAD = 256


def _bdot(a, b):
    return jnp.dot(a.astype(BF16), b.astype(BF16), preferred_element_type=F32)


def _bdot_nt(a, b):
    return lax.dot_general(a.astype(BF16), b.astype(BF16), (((1,), (1,)), ((), ())),
                           preferred_element_type=F32)


def _bdot_tn(a, b):
    return lax.dot_general(a.astype(BF16), b.astype(BF16), (((0,), (0,)), ((), ())),
                           preferred_element_type=F32)


def _split_bf16(x, parts):
    out = []
    rem = x
    for _ in range(parts):
        p = rem.astype(BF16)
        out.append(p)
        rem = rem - p.astype(F32)
    return out


def _dot_exact_rhs(x, m_bf16, parts):
    acc = None
    for p in _split_bf16(x, parts):
        t = jnp.dot(p, m_bf16, preferred_element_type=F32)
        acc = t if acc is None else acc + t
    return acc


def _exact_lhs_dot(m_bf16, x, parts):
    acc = None
    for p in _split_bf16(x, parts):
        t = jnp.dot(m_bf16, p, preferred_element_type=F32)
        acc = t if acc is None else acc + t
    return acc


def _rms_mod(x, g, shift, scale):
    y = x * lax.rsqrt(jnp.mean(x * x, axis=-1, keepdims=True) + RMS_EPS) * g
    return y * (1.0 + scale) + shift


def _softplus(z):
    return jnp.maximum(z, 0.0) + jnp.log1p(jnp.exp(-jnp.abs(z)))


def _ada_kernel(c_ref, w_ref, b_ref, o_ref):
    c = c_ref[...]
    cond = c * jax.nn.sigmoid(c)
    o_ref[0] = _bdot(cond, w_ref[0]) + b_ref[0]


def _ada_call(c_pad, ada_w, ada_b):
    depth, dm, n9 = ada_w.shape
    tn = n9 // 4
    rows = c_pad.shape[0]
    return pl.pallas_call(
        _ada_kernel,
        grid=(depth, n9 // tn),
        in_specs=[
            pl.BlockSpec((rows, dm), lambda l, j: (0, 0)),
            pl.BlockSpec((1, dm, tn), lambda l, j: (l, 0, j)),
            pl.BlockSpec((1, 1, tn), lambda l, j: (l, 0, j)),
        ],
        out_specs=pl.BlockSpec((1, rows, tn), lambda l, j: (l, 0, j)),
        out_shape=jax.ShapeDtypeStruct((depth, rows, n9), F32),
        compiler_params=pltpu.CompilerParams(
            dimension_semantics=("arbitrary", "arbitrary"), vmem_limit_bytes=VMEM_LIMIT),
        name="ada_mod",
    )(c_pad, ada_w, ada_b.reshape(depth, 1, n9))


def _row_parts(ref, parts):
    rows = ref.shape[0] // parts
    return [ref[k * rows:(k + 1) * rows, :] for k in range(parts)]


def _store_row_parts(ref, vals):
    rows = ref.shape[0] // len(vals)
    for k, v in enumerate(vals):
        ref[k * rows:(k + 1) * rows, :] = v


def _ffn_half_step(xs, mod_ref, g_ref, wi_ref, wo_ref, fg_ref, *, mod_row, final_norm):
    dff = wo_ref.shape[0]
    shift, scale = mod_ref[0, mod_row:mod_row + 1, :], mod_ref[0, mod_row + 1:mod_row + 2, :]
    hs = [_rms_mod(x, g_ref[...], shift, scale).astype(BF16) for x in xs]
    gus = [jnp.dot(h, wi_ref[...], preferred_element_type=F32) for h in hs]
    acts = [((gu[:, :dff] * jax.nn.sigmoid(gu[:, :dff])) * gu[:, dff:]).astype(BF16) for gu in gus]
    ys = [jnp.dot(a, wo_ref[...], preferred_element_type=F32) for a in acts]
    outs = [x + (0.5 * (1.0 + mod_ref[0, mod_row + 2:mod_row + 3, :])) * y for x, y in zip(xs, ys)]
    if final_norm:
        outs = [o * lax.rsqrt(jnp.mean(o * o, axis=-1, keepdims=True) + RMS_EPS) * fg_ref[...] for o in outs]
    return outs


def _ffn_kernel(x_ref, mod_ref, g_ref, wi_ref, wo_ref, fg_ref, o_ref, *, mod_row, final_norm):
    outs = _ffn_half_step(_row_parts(x_ref, ROW_SPLIT), mod_ref, g_ref, wi_ref, wo_ref, fg_ref,
                          mod_row=mod_row, final_norm=final_norm)
    _store_row_parts(o_ref, outs)


def _resident(shape):
    return pl.BlockSpec(shape, lambda *_: (0,) * len(shape), pipeline_mode=pl.Buffered(1))


def _resident_at(arr, lead):
    tail = arr.shape[len(lead):]
    return pl.BlockSpec((None,) * len(lead) + tail, lambda *_: tuple(lead) + (0,) * len(tail),
                        pipeline_mode=pl.Buffered(1))


def _ffn_call(x2, mod_l, norm_g, w_in, w_out, final_g, *, lead, seq, mod_row, final_norm, tm, regroup=None):
    n, dm = x2.shape
    steps = n // tm
    kern = functools.partial(_ffn_kernel, mod_row=mod_row, final_norm=final_norm)
    in_specs = [
        pl.BlockSpec((tm, dm), lambda i: (i, 0)),
        pl.BlockSpec((1, 9, dm), lambda i: ((i * tm) // seq, 0, 0)),
        _resident((1, dm)),
        _resident_at(w_in, lead),
        _resident_at(w_out, lead),
        _resident((1, dm)),
    ]
    out_specs = [pl.BlockSpec((tm, dm), lambda i: (i, 0))]
    out_shape = [jax.ShapeDtypeStruct((n, dm), F32)]
    extra = []
    if regroup is not None:
        src, segments, out_cols = regroup
        depth, rows, cols = src.shape
        blk = rows // steps
        assert rows % steps == 0 and blk % BF16_SUBLANES == 0 and out_cols % LANES == 0
        for l in range(depth):
            in_specs.append(pl.BlockSpec((None, blk, cols), lambda i, l=l: (l, i, 0)))
            out_specs.append(pl.BlockSpec((blk, out_cols), lambda i: (i, 0)))
            out_shape.append(jax.ShapeDtypeStruct((rows, out_cols), BF16))
            extra.append(src)
        kern = _with_cast_riders(kern, 6, 1, depth, fn=functools.partial(_regroup_cols, segments))
    outs = pl.pallas_call(
        kern,
        grid=(steps,),
        in_specs=in_specs,
        out_specs=out_specs,
        out_shape=out_shape,
        compiler_params=pltpu.CompilerParams(
            dimension_semantics=("parallel",), vmem_limit_bytes=VMEM_LIMIT),
        name="ffn",
    )(x2, mod_l, norm_g, w_in, w_out, final_g, *extra)
    return outs[0], list(outs[1:])


def _mixin_kernel(x_ref, mod_ref, g_ref, w_ref, mu_ref, cw_ref, cb_ref,
                  xs_ref, qk_ref, pm_ref, pg_ref, prev_ref, xbuf, *, tiles_per_seq):
    tm = x_ref.shape[0]
    KC = cw_ref.shape[0]
    first = (pl.program_id(0) % tiles_per_seq) == 0
    c1 = qk_ref.shape[1]
    c2 = c1 + xs_ref.shape[1]
    c3 = c2 + pm_ref.shape[1]
    c4 = c3 + pg_ref.shape[1]

    @pl.when(first)
    def _():
        prev_ref[...] = jnp.zeros_like(prev_ref)
        xbuf[0:SUBLANES, :] = jnp.zeros((SUBLANES, xbuf.shape[1]), F32)

    @pl.when(jnp.logical_not(first))
    def _():
        xbuf[0:SUBLANES, :] = xbuf[tm:tm + SUBLANES, :]

    h = _rms_mod(x_ref[...], g_ref[...], mod_ref[0, 3:4, :], mod_ref[0, 4:5, :]).astype(BF16)

    pr = jnp.dot(h, w_ref[:, c1:c2], preferred_element_type=F32)
    row = lax.broadcasted_iota(jnp.int32, (tm, 1), 0)
    prev = jnp.where(row == 0, prev_ref[...], pltpu.roll(pr, 1, axis=0))
    prev_ref[...] = pr[tm - 1:tm, :]
    xs_ref[...] = pr + (prev - pr) * mu_ref[...]

    xbuf[SUBLANES:SUBLANES + tm, :] = jnp.dot(h, w_ref[:, 0:c1], preferred_element_type=F32)
    acc = cb_ref[...] + cw_ref[KC - 1:KC, :] * xbuf[SUBLANES:SUBLANES + tm, :]
    for j in range(KC - 1):
        off = SUBLANES - (KC - 1) + j
        acc = acc + cw_ref[j:j + 1, :] * xbuf[off:off + tm, :]
    qk_ref[...] = acc * jax.nn.sigmoid(acc)

    pm_ref[...] = jnp.dot(h, w_ref[:, c2:c3], preferred_element_type=F32)
    pg_ref[...] = jnp.dot(h, w_ref[:, c3:c4], preferred_element_type=F32)


def _mixin_call(x2, mod_l, norm_g, w_all, widths, mu, conv_w, conv_b, *, seq, tm):
    n, dm = x2.shape
    assert seq % tm == 0 and conv_w.shape[0] - 1 <= SUBLANES and sum(widths) == w_all.shape[-1]
    kern = functools.partial(_mixin_kernel, tiles_per_seq=seq // tm)
    return pl.pallas_call(
        kern,
        grid=(n // tm,),
        in_specs=[
            pl.BlockSpec((tm, dm), lambda i: (i, 0)),
            pl.BlockSpec((1, 9, dm), lambda i: ((i * tm) // seq, 0, 0)),
            _resident((1, dm)),
            _resident(w_all.shape),
            _resident(mu.shape), _resident(conv_w.shape), _resident(conv_b.shape),
        ],
        out_specs=[pl.BlockSpec((tm, w), lambda i: (i, 0)) for w in widths],
        out_shape=[jax.ShapeDtypeStruct((n, w), F32) for w in widths],
        scratch_shapes=[pltpu.VMEM((1, widths[0]), F32), pltpu.VMEM((tm + SUBLANES, widths[1]), F32)],
        compiler_params=pltpu.CompilerParams(
            dimension_semantics=("arbitrary",), vmem_limit_bytes=VMEM_LIMIT),
        name="mix_in",
    )(x2, mod_l, norm_g, w_all, mu, conv_w, conv_b)


def _rwkv_kernel(xs_ref, w0_ref, wup_ref, a0_ref, aup_ref, gup_ref, kk_ref, ka_ref, rk_ref,
                 lng_ref, lnb_ref, o_ref,
                 s_ref, r_s, y_s, bonus_s, g_s, mx_s, n0_s, pc_s, side_work=iter(())):
    tc = xs_ref.shape[0]
    width = o_ref.shape[1]
    C = RWKV_CHUNK
    HD = RWKV_HEAD_DIM
    HG = RWKV_GROUP_HEADS
    GL = HG * HD
    n_groups = width // GL
    shift = HD.bit_length() - 1
    i = pl.program_id(1)

    @pl.when(i == 0)
    def _():
        s_ref[...] = jnp.zeros_like(s_ref)

    rg = lax.broadcasted_iota(jnp.int32, (GL, GL), 0)
    cg = lax.broadcasted_iota(jnp.int32, (GL, GL), 1)
    blockmask = (rg >> shift) == (cg >> shift)
    eye = rg == cg
    ones_bd = jnp.where(blockmask, 1.0, 0.0).astype(BF16)

    def segsum(x):
        return jnp.concatenate(
            [_dot_exact_rhs(x[:, g * GL:(g + 1) * GL], ones_bd, 2) for g in range(n_groups)], axis=1)

    def token_quantities(rows):
        r = xs_ref[rows, 0:width]
        k = xs_ref[rows, width:2 * width]
        v = xs_ref[rows, 2 * width:3 * width]
        low = xs_ref[rows, 3 * width:3 * width + RWKV_LOWRANK_PAD]
        w = w0_ref[...] + _bdot(jnp.tanh(low), wup_ref[...])
        ld = (-math.exp(-0.5)) * jax.nn.sigmoid(w)
        asig = jax.nn.sigmoid(a0_ref[...] + _bdot(low, aup_ref[...]))
        g_s[rows, :] = _bdot(jax.nn.sigmoid(low), gup_ref[...])
        kk = k * kk_ref[...]
        kk = kk * lax.rsqrt(jnp.maximum(segsum(kk * kk), L2_EPS * L2_EPS))
        k2 = k * (1.0 + (asig - 1.0) * ka_ref[...])
        bonus_s[rows, :] = segsum(r * k2 * rk_ref[...]) * v
        return dict(r=r, ld=ld, k=k2, v=v, a=-kk, b=kk * asig)

    lane = lax.broadcasted_iota(jnp.int32, (C, GL), 1)
    tok = lax.broadcasted_iota(jnp.int32, (C, GL), 0)
    strict = (lane & (HD - 1)) < tok
    incl = (lane & (HD - 1)) <= tok
    headmask = [(lane >> shift) == h for h in range(HG)]
    tri = (lax.broadcasted_iota(jnp.int32, (C, C), 1) <= lax.broadcasted_iota(jnp.int32, (C, C), 0))
    tri = jnp.where(tri, 1.0, 0.0).astype(BF16)

    def bd(x):
        return jnp.where(blockmask, jnp.concatenate([x] * HG, axis=0), 0.0).astype(BF16)

    levels = C.bit_length() - 1

    def prepare_stages(c_first):
        inst = []
        tq = token_quantities(slice(c_first * C, (c_first + RWKV_PREP_UNROLL) * C))
        yield
        for u in range(RWKV_PREP_UNROLL):
            c = c_first + u
            sl = slice(c * C, (c + 1) * C)
            loc = slice(u * C, (u + 1) * C)
            ld = tq['ld'][loc]
            lp = _exact_lhs_dot(tri, ld, 3)
            lpc = lp[C - 1:C, :]
            e_lp = jnp.exp(lp)
            e_nlp = jnp.exp(-lp)
            e_end = jnp.exp(lpc - lp)
            rt = tq['r'][loc] * e_lp
            at = tq['a'][loc] * jnp.exp(lp - ld)
            bb = tq['b'][loc]
            kc = tq['k'][loc]
            bt = bb * e_nlp
            kt = kc * e_nlp
            bh = bb * e_end
            kh = kc * e_end
            vv = tq['v'][loc]
            pc = jnp.exp(lpc)
            for g in range(n_groups):
                gs = slice(g * GL, (g + 1) * GL)
                inst.append(dict(sl=sl, gs=gs, idx=c * n_groups + g, at=at[:, gs], rt=rt[:, gs], v=vv[:, gs],
                                 bt=bt[:, gs], kt=kt[:, gs], bh=bh[:, gs], kh=kh[:, gs], pc=pc[:, gs]))
        for t in inst:
            lhs = jnp.concatenate([t['at'], t['rt']], axis=0)
            rows = [jnp.where(headmask[h], t['bt'], 0.0) for h in range(HG)]
            rows += [jnp.where(headmask[h], t['kt'], 0.0) for h in range(HG)]
            gram = _bdot_nt(lhs, jnp.concatenate(rows, axis=0))
            t['a_ab'] = jnp.where(strict, gram[:C, :GL], 0.0)
            t['a_ak'] = jnp.where(strict, gram[:C, GL:], 0.0)
            t['a_rb'] = jnp.where(incl, gram[C:, :GL], 0.0).astype(BF16)
            t['a_rk'] = jnp.where(incl, gram[C:, GL:], 0.0)
        yield
        for t in inst:
            t['bdv'] = bd(t['v'])
            both = _bdot(jnp.concatenate([t['a_ak'], t['a_rk']], axis=0), t['bdv'])
            t['akv'] = both[:C]
            t['arkv'] = both[C:]
        yield
        for t in inst:
            t['tr'] = t['a_ab']
            t['nm'] = jnp.dot(t['a_ab'].astype(BF16), bd(t['a_ab']), preferred_element_type=F32)
        yield
        for lvl in range(1, levels):
            for t in inst:
                if lvl + 1 < levels:
                    both = _bdot(jnp.concatenate([t['tr'], t['nm']], axis=0), bd(t['nm']))
                    t['tr'] = t['tr'] + t['nm'] + both[:C]
                    t['nm'] = both[C:]
                else:
                    t['tr'] = t['tr'] + t['nm'] + _bdot(t['tr'], bd(t['nm']))
            yield
        for t in inst:
            wu = _bdot(t['tr'], jnp.concatenate([bd(t['at']), bd(t['akv'])], axis=1))
            t['w'] = t['at'] + wu[:, :GL]
            t['u0'] = t['akv'] + wu[:, GL:]
        yield
        for t in inst:
            ry = jnp.dot(t['a_rb'], jnp.concatenate([bd(t['w']), bd(t['u0'])], axis=1),
                         preferred_element_type=F32)
            r_s[t['sl'], t['gs']] = t['rt'] + ry[:, :GL]
            y_s[t['sl'], t['gs']] = ry[:, GL:] + t['arkv']
        yield
        for t in inst:
            mn = _bdot_tn(t['bh'], jnp.concatenate([t['w'], t['u0']], axis=1))
            mx_s[t['idx']] = jnp.where(blockmask, mn[:, :GL], 0.0).astype(BF16)
            n0_s[t['idx']] = jnp.where(blockmask, mn[:, GL:] + _bdot_tn(t['kh'], t['v']), 0.0)
            pc_s[t['idx']] = jnp.broadcast_to(
                jnp.sum(jnp.where(eye, jnp.broadcast_to(t['pc'], (GL, GL)), 0.0), axis=1, keepdims=True),
                (GL, GL))

    def advance(c):
        sl = slice(c * C, (c + 1) * C)
        for g in range(n_groups):
            gs = slice(g * GL, (g + 1) * GL)
            idx = c * n_groups + g
            s_old = s_ref[g]
            s_bf = s_old.astype(BF16)
            y_s[sl, gs] = y_s[sl, gs] + jnp.dot(r_s[sl, gs].astype(BF16), s_bf, preferred_element_type=F32)
            s_ref[g] = pc_s[idx] * s_old + n0_s[idx] + jnp.dot(mx_s[idx], s_bf, preferred_element_type=F32)

    def finish(rows):
        y = y_s[rows, :]
        inv = 1.0 / HD
        yc = y - segsum(y) * inv
        var = segsum(yc * yc) * inv
        yn = yc * lax.rsqrt(var + RWKV_GN_EPS) * lng_ref[...] + lnb_ref[...]
        o_ref[rows, :] = (yn + bonus_s[rows, :]) * g_s[rows, :]

    n_chunks = tc // C
    pending = []
    done_rows = None
    for c_first in range(0, n_chunks, RWKV_PREP_UNROLL):
        stages = prepare_stages(c_first)
        for k, _ in enumerate(stages):
            if pending and k % 2 == 0:
                advance(pending.pop(0))
            elif not pending and done_rows is not None:
                finish(done_rows)
                done_rows = None
            next(side_work, None)
        for c in pending:
            advance(c)
        if done_rows is not None:
            finish(done_rows)
        pending = list(range(c_first, c_first + RWKV_PREP_UNROLL))
        done_rows = slice(c_first * C, (c_first + RWKV_PREP_UNROLL) * C)
    for c in pending:
        advance(c)
    finish(done_rows)
    for _ in side_work:
        pass


BF16_SUBLANES = 16


def _cast_blocking(rows, cols, steps):
    for cb in range(1, steps + 1):
        rb = steps // cb
        if (steps % cb == 0 and rows % rb == 0 and cols % cb == 0
                and (rows // rb) % BF16_SUBLANES == 0 and (cols // cb) % LANES == 0):
            return rb, cb
    raise ValueError((rows, cols, steps))


def _cast_rider_specs(casts, nt):
    in_specs, out_specs, out_shapes = [], [], []
    for arr, lead, (rb, cb) in casts:
        rows, cols = arr.shape[-2:]
        blk = (rows // rb, cols // cb)
        in_specs.append(pl.BlockSpec(
            (None,) * len(lead) + blk,
            lambda b, i, lead=lead, cb=cb: tuple(lead) + ((b * nt + i) // cb, (b * nt + i) % cb)))
        out_specs.append(pl.BlockSpec(blk, lambda b, i, cb=cb: ((b * nt + i) // cb, (b * nt + i) % cb)))
        out_shapes.append(jax.ShapeDtypeStruct((rows, cols), BF16))
    return in_specs, out_specs, out_shapes


def _cast_block(src, dst):
    dst[...] = src[...].astype(dst.dtype)


def _with_cast_riders(body, n_in, n_out, n_cast, fn=_cast_block):
    def kern(*refs):
        ins = refs[:n_in]
        cast_ins = refs[n_in:n_in + n_cast]
        outs = refs[n_in + n_cast:n_in + n_cast + n_out]
        cast_outs = refs[n_in + n_cast + n_out:n_in + 2 * n_cast + n_out]
        scratch = refs[n_in + 2 * n_cast + n_out:]
        for src, dst in zip(cast_ins, cast_outs):
            fn(src, dst)
        body(*ins, *outs, *scratch)
    return kern


def _regroup_cols(segments, src, dst):
    dst[...] = jnp.zeros(dst.shape, dst.dtype)
    x = src[...]
    for s, d, w in segments:
        dst[:, d:d + w] = x[:, s:s + w].astype(dst.dtype)


def _rwkv_specs(ncols, *, seq, tc):
    width = RWKV_HEADS * RWKV_HEAD_DIM
    gl = RWKV_GROUP_HEADS * RWKV_HEAD_DIM
    nt = seq // tc
    assert seq % tc == 0 and tc % (RWKV_CHUNK * RWKV_PREP_UNROLL) == 0 and width % gl == 0
    nblk = (tc // RWKV_CHUNK) * (width // gl)
    vec = _resident((1, width))
    lowmat = _resident((RWKV_LOWRANK_PAD, width))
    tile = pltpu.VMEM((tc, width), F32)
    in_specs = [pl.BlockSpec((tc, ncols), lambda b, i: (b * nt + i, 0)),
                vec, lowmat, vec, lowmat, lowmat, vec, vec, vec, vec, vec]
    out_spec = pl.BlockSpec((tc, width), lambda b, i: (b * nt + i, 0))
    scratch = [
        pltpu.VMEM((width // gl, gl, gl), F32),
        tile, tile, tile, tile,
        pltpu.VMEM((nblk, gl, gl), BF16),
        pltpu.VMEM((nblk, gl, gl), F32),
        pltpu.VMEM((nblk, gl, gl), F32),
    ]
    return in_specs, out_spec, scratch


def _cummax_rows(x):
    n = x.shape[0]
    row = lax.broadcasted_iota(jnp.int32, x.shape, 0)
    s = 1
    while s < n:
        x = jnp.maximum(x, jnp.where(row >= s, pltpu.roll(x, s, axis=0), -jnp.inf))
        s *= 2
    return x


def _mlstm_stages(qk_s, p_ref, gbi_ref, gbf_ref, lng_ref, o_ref, cn_ref, m_ref):
    tcm = p_ref.shape[0]
    NH, DH, L = MLSTM_HEADS, MLSTM_HEAD_DIM, MLSTM_CHUNK
    wd = NH * DH
    i = pl.program_id(1)

    @pl.when(i == 0)
    def _():
        cn_ref[...] = jnp.zeros_like(cn_ref)
        m_ref[...] = jnp.full(m_ref.shape, -jnp.inf, F32)

    rr = lax.broadcasted_iota(jnp.int32, (L, L), 0)
    cc = lax.broadcasted_iota(jnp.int32, (L, L), 1)
    causal = cc <= rr
    tri = jnp.where(causal, 1.0, 0.0).astype(BF16)
    eye_l = jnp.where(cc == rr, 1.0, 0.0).astype(BF16)
    eye_h = eye_l[0:2 * SUBLANES, :]
    ones_v = jnp.ones((L, DH), BF16)
    scale = DH ** -0.5
    heads = range(NH)

    def chunk(c):
        sl = slice(c * L, (c + 1) * L)
        li = p_ref[sl, 2 * wd:2 * wd + LANES] + gbi_ref[...]
        lf = -_softplus(-(p_ref[sl, 2 * wd + LANES:2 * wd + 2 * LANES] + gbf_ref[...]))
        b_col = _exact_lhs_dot(tri, lf, 3)
        q = [(qk_s[sl, h * DH:(h + 1) * DH] * scale).astype(BF16) for h in heads]
        k = [qk_s[sl, wd + h * DH:wd + (h + 1) * DH].astype(BF16) for h in heads]
        s = [lax.dot_general(q[h], k[h], (((1,), (1,)), ((), ())), preferred_element_type=F32) for h in heads]
        yield
        xcol = li - b_col
        m_prev = m_ref[...]
        zcol = jnp.maximum(m_prev, _cummax_rows(xcol))
        b_end = b_col[L - 1:L, :]
        m_new = jnp.maximum(b_end + m_prev, jnp.max(b_end + xcol, axis=0, keepdims=True))
        keep = jnp.exp(b_end + m_prev - m_new)
        m_ref[...] = m_new
        wshift = b_end - m_new
        xrow = None
        for part in _split_bf16(xcol, 3):
            t = lax.dot_general(eye_h, part, (((1,), (1,)), ((), ())), preferred_element_type=F32)
            xrow = t if xrow is None else xrow + t
        vx = [jnp.concatenate([p_ref[sl, h * DH:(h + 1) * DH].astype(BF16), ones_v], axis=1)
              for h in heads]
        cn = [cn_ref[h] for h in heads]
        kt = [lax.dot_general(eye_l, k[h], (((1,), (1,)), ((), ())), preferred_element_type=F32) for h in heads]
        qc = [jnp.dot(q[h], cn[h].astype(BF16), preferred_element_type=F32) for h in heads]
        yield
        zb = [jnp.broadcast_to(zcol[:, h:h + 1], (L, DH)) for h in heads]
        bb = [jnp.broadcast_to(b_col[:, h:h + 1], (L, DH)) for h in heads]
        pw = [(s[h] * jnp.exp(jnp.where(causal, xrow[h:h + 1, :] - zb[h], -jnp.inf))).astype(BF16) for h in heads]
        nd = [jnp.dot(pw[h], vx[h], preferred_element_type=F32) for h in heads]
        for h in heads:
            wrow = jnp.exp(xrow[h:h + 1, :] + wshift[:, h:h + 1])
            cn_ref[h] = keep[:, h:h + 1] * cn[h] + jnp.dot((kt[h] * wrow).astype(BF16), vx[h],
                                                            preferred_element_type=F32)
        yield
        for h in heads:
            hs = slice(h * DH, (h + 1) * DH)
            inter = jnp.exp(m_prev[:, h:h + 1] - zb[h])
            num = nd[h][:, :DH] + inter * qc[h][:, :DH]
            den = nd[h][:, DH:] + inter * qc[h][:, DH:]
            hh = num / jnp.maximum(jnp.abs(den), jnp.exp(-(bb[h] + zb[h])))
            hc = hh - jnp.mean(hh, axis=1, keepdims=True)
            hn = hc * lax.rsqrt(jnp.mean(hc * hc, axis=1, keepdims=True) + MLSTM_NORM_EPS)
            og = jax.nn.sigmoid(p_ref[sl, wd + h * DH:wd + (h + 1) * DH])
            o_ref[sl, hs] = og * (hn * lng_ref[:, hs])
        yield

    for c in range(tcm // L):
        yield from chunk(c)


def _mlstm_specs(ncols, *, seq, tc):
    nh, dh = MLSTM_HEADS, MLSTM_HEAD_DIM
    wd = nh * dh
    nt = seq // tc
    assert seq % tc == 0 and tc % MLSTM_CHUNK == 0
    in_specs = [
        pl.BlockSpec((tc, 2 * wd), lambda b, i: (b * nt + i, 0)),
        pl.BlockSpec((tc, ncols), lambda b, i: (b * nt + i, 0)),
        _resident((1, LANES)),
        _resident((1, LANES)),
        _resident((1, wd)),
    ]
    out_spec = pl.BlockSpec((tc, wd), lambda b, i: (b * nt + i, 0))
    scratch = [pltpu.VMEM((nh, dh, 2 * dh), F32), pltpu.VMEM((1, LANES), F32)]
    return in_specs, out_spec, scratch


N_RWKV_IN, N_RWKV_SCRATCH, N_MLSTM_IN = 11, 8, 5


def _mixers_kernel(*refs):
    r_in = refs[:N_RWKV_IN]
    m_in = refs[N_RWKV_IN:N_RWKV_IN + N_MLSTM_IN]
    yr_ref, ym_ref = refs[N_RWKV_IN + N_MLSTM_IN:N_RWKV_IN + N_MLSTM_IN + 2]
    scratch = refs[N_RWKV_IN + N_MLSTM_IN + 2:]
    _rwkv_kernel(*r_in, yr_ref, *scratch[:N_RWKV_SCRATCH],
                 side_work=_mlstm_stages(*m_in, ym_ref, *scratch[N_RWKV_SCRATCH:]))


def _mixers_call(rwkv_args, mlstm_args, *, batch, seq, tc, casts=()):
    n = rwkv_args[0].shape[0]
    nt = seq // tc
    r_in, r_out, r_scratch = _rwkv_specs(rwkv_args[0].shape[1], seq=seq, tc=tc)
    m_in, m_out, m_scratch = _mlstm_specs(mlstm_args[1].shape[1], seq=seq, tc=tc)
    assert (len(r_in), len(r_scratch), len(m_in)) == (N_RWKV_IN, N_RWKV_SCRATCH, N_MLSTM_IN)
    casts = [(arr, lead, _cast_blocking(*arr.shape[-2:], batch * nt)) for arr, lead in casts]
    cast_in, cast_out, cast_shapes = _cast_rider_specs(casts, nt)
    outs = pl.pallas_call(
        _with_cast_riders(_mixers_kernel, len(r_in) + len(m_in), 2, len(casts)),
        grid=(batch, nt),
        in_specs=r_in + m_in + cast_in,
        out_specs=[r_out, m_out] + cast_out,
        out_shape=[jax.ShapeDtypeStruct((n, RWKV_HEADS * RWKV_HEAD_DIM), F32),
                   jax.ShapeDtypeStruct((n, MLSTM_HEADS * MLSTM_HEAD_DIM), F32)] + cast_shapes,
        scratch_shapes=r_scratch + m_scratch,
        compiler_params=pltpu.CompilerParams(
            dimension_semantics=("parallel", "arbitrary"), vmem_limit_bytes=VMEM_LIMIT),
        name="mixers",
    )(*rwkv_args, *mlstm_args, *[c[0] for c in casts])
    return outs[0], outs[1], list(outs[2:])


def _mixout_ffn_kernel(x_ref, yr_ref, ym_ref, pg_ref, mod_ref, wr_ref, wm_ref, wo_ref,
                       g_ref, wi_ref, wf_ref, fg_ref, o_ref, *, final_norm):
    dm = x_ref.shape[1]
    zrs = [jnp.dot(y.astype(BF16), wr_ref[...], preferred_element_type=F32) for y in _row_parts(yr_ref, ROW_SPLIT)]
    zms = [jnp.dot(y.astype(BF16), wm_ref[...], preferred_element_type=F32) for y in _row_parts(ym_ref, ROW_SPLIT)]
    mixes = [(jax.nn.sigmoid(pg[:, 0:dm]) * zr + jax.nn.sigmoid(pg[:, dm:2 * dm]) * zm).astype(BF16)
             for pg, zr, zm in zip(_row_parts(pg_ref, ROW_SPLIT), zrs, zms)]
    outs = [jnp.dot(m, wo_ref[...], preferred_element_type=F32) for m in mixes]
    xs = [x + (1.0 + mod_ref[0, 5:6, :]) * o for x, o in zip(_row_parts(x_ref, ROW_SPLIT), outs)]
    _store_row_parts(o_ref, _ffn_half_step(xs, mod_ref, g_ref, wi_ref, wf_ref, fg_ref, mod_row=6,
                                           final_norm=final_norm))


def _mixout_ffn_call(x2, y_r, y_m, p_g, mod_l, w_r, w_m, w_o, norm_g, w_in, w_out, final_g, *,
                     layer, seq, final_norm, tm):
    n, dm = x2.shape
    rowblk = lambda cols: pl.BlockSpec((tm, cols), lambda i: (i, 0))
    kern = functools.partial(_mixout_ffn_kernel, final_norm=final_norm)
    return pl.pallas_call(
        kern,
        grid=(n // tm,),
        in_specs=[
            rowblk(dm), rowblk(y_r.shape[1]), rowblk(y_m.shape[1]), rowblk(p_g.shape[1]),
            pl.BlockSpec((1, 9, dm), lambda i: ((i * tm) // seq, 0, 0)),
            _resident_at(w_r, (layer,)), _resident_at(w_m, (layer,)), _resident_at(w_o, (layer,)),
            _resident((1, dm)), _resident_at(w_in, ()), _resident_at(w_out, ()),
            _resident((1, dm)),
        ],
        out_specs=rowblk(dm),
        out_shape=jax.ShapeDtypeStruct((n, dm), F32),
        compiler_params=pltpu.CompilerParams(
            dimension_semantics=("parallel",), vmem_limit_bytes=VMEM_LIMIT),
        name="mix_out_ffn",
    )(x2, y_r, y_m, p_g, mod_l, w_r, w_m, w_o, norm_g, w_in, w_out, final_g)


def _pad_cols(a, n):
    return jnp.pad(a, ((0, 0), (0, n - a.shape[1])))


def _row(a, n=None):
    a = a.reshape(1, -1)
    return a if n is None else _pad_cols(a, n)


def _lowrank_pad(mat, row0):
    out = jnp.zeros((RWKV_LOWRANK_PAD, mat.shape[1]), F32)
    return lax.dynamic_update_slice(out, mat, (row0, 0)).astype(BF16)


def kernel(x, c, ada_w, ada_b, norm_g, ffn_w_in, ffn_w_out, mix_w_in, rwkv_mu, rwkv_w0, rwkv_w_up, rwkv_a0, rwkv_a_up, rwkv_g_up, rwkv_k_k, rwkv_k_a, rwkv_r_k, rwkv_ln_g, rwkv_ln_b, mlstm_conv_w, mlstm_conv_b, mlstm_gate_b, mlstm_ln_g, branch_w_rwkv, branch_w_mlstm, mix_w_out, final_g):
    batch, seq, dm = x.shape
    depth = ada_w.shape[0]
    n = batch * seq
    rw = RWKV_HEADS * RWKV_HEAD_DIM
    mw = MLSTM_HEADS * MLSTM_HEAD_DIM
    rwkv_cols = 3 * rw + RWKV_DECAY_RANK + RWKV_A_RANK + RWKV_GATE_RANK
    mlstm_cols = 4 * mw + 2 * MLSTM_HEADS
    rwkv_pad = 3 * rw + RWKV_LOWRANK_PAD

    tm_ffn = min(512, seq)
    tm_mix = min(512, seq)
    tc_mixers = min(512, seq)

    c_pad = jnp.pad(c, ((0, SUBLANES - batch % SUBLANES if batch % SUBLANES else 0), (0, 0)))
    mod = _ada_call(c_pad, ada_w, ada_b)[:, :batch].reshape(depth, batch, 9, dm)

    v0 = rwkv_cols + 2 * mw
    gate0 = rwkv_cols + 4 * mw
    d_r = 2 * mw
    d_m = d_r + rwkv_pad
    d_g = d_m + 2 * mw + 2 * LANES
    mix_segments = [
        (rwkv_cols, 0, 2 * mw),
        (0, d_r, rwkv_cols),
        (v0, d_m, 2 * mw),
        (gate0, d_m + 2 * mw, MLSTM_HEADS),
        (gate0 + MLSTM_HEADS, d_m + 2 * mw + LANES, MLSTM_HEADS),
        (rwkv_cols + mlstm_cols, d_g, 2 * dm)]
    mix_widths = [rwkv_pad, 2 * mw, 2 * mw + 2 * LANES, 2 * dm]

    branch_r_bf = branch_w_rwkv.astype(BF16)
    branch_m_bf = branch_w_mlstm.astype(BF16)
    mix_w_out_bf = mix_w_out.astype(BF16)
    final_row = _row(final_g)

    ffn_in_bf = {(0, 0): ffn_w_in[0, 0].astype(BF16)}
    ffn_out_bf = {(0, 0): ffn_w_out[0, 0].astype(BF16)}

    x2 = x.reshape(n, dm)
    for l in range(depth):
        mod_l = mod[l]
        x2, regrouped = _ffn_call(
            x2, mod_l, _row(norm_g[l, 0]), ffn_in_bf[(l, 0)], ffn_out_bf[(l, 0)], final_row,
            lead=(), seq=seq, mod_row=0, final_norm=False, tm=tm_ffn,
            regroup=(mix_w_in, mix_segments, sum(mix_widths)) if l == 0 else None)
        if l == 0:
            mix_w_all = regrouped
        later = [(l, 1)] + ([(l + 1, 0)] if l + 1 < depth else [])

        xs_r, qk_m, p_m, p_g = _mixin_call(
            x2, mod_l, _row(norm_g[l, 1]), mix_w_all[l], mix_widths, _row(rwkv_mu[l], rwkv_pad),
            mlstm_conv_w[l], _row(mlstm_conv_b[l]), seq=seq, tm=tm_mix)

        rwkv_args = [
            xs_r, _row(rwkv_w0[l]),
            _lowrank_pad(rwkv_w_up[l], 0), _row(rwkv_a0[l]),
            _lowrank_pad(rwkv_a_up[l], RWKV_DECAY_RANK),
            _lowrank_pad(rwkv_g_up[l], RWKV_DECAY_RANK + RWKV_A_RANK),
            _row(rwkv_k_k[l]), _row(rwkv_k_a[l]), _row(rwkv_r_k[l]), _row(rwkv_ln_g[l]), _row(rwkv_ln_b[l])]
        mlstm_args = [
            qk_m, p_m, _row(mlstm_gate_b[l, :MLSTM_HEADS], LANES), _row(mlstm_gate_b[l, MLSTM_HEADS:], LANES),
            _row(mlstm_ln_g[l])]
        y_r, y_m, cast = _mixers_call(
            rwkv_args, mlstm_args, batch=batch, seq=seq, tc=tc_mixers,
            casts=[(ffn_w_in, lf) for lf in later] + [(ffn_w_out, lf) for lf in later])
        ffn_in_bf.update(zip(later, cast[:len(later)]))
        ffn_out_bf.update(zip(later, cast[len(later):]))

        x2 = _mixout_ffn_call(
            x2, y_r, y_m, p_g, mod_l, branch_r_bf, branch_m_bf, mix_w_out_bf, _row(norm_g[l, 2]),
            ffn_in_bf[(l, 1)], ffn_out_bf[(l, 1)], final_row, layer=l, seq=seq, final_norm=(l == depth - 1),
            tm=tm_ffn)
    return x2.reshape(batch, seq, dm)
```

```python
import functools
import math

import jax
import jax.numpy as jnp
from jax import lax
from jax.experimental import pallas as pl
from jax.experimental.pallas import tpu as pltpu

F32 = jnp.float32
BF16 = jnp.bfloat16

RWKV_HEADS = 8
RWKV_HEAD_DIM = 64
RWKV_DECAY_RANK = 32
RWKV_A_RANK = 32
RWKV_GATE_RANK = 96
MLSTM_HEADS = 4
MLSTM_HEAD_DIM = 128
MLSTM_CONV = 4
MLSTM_CHUNK = 128
RMS_EPS = 1e-6
RWKV_GN_EPS = 64e-5
MLSTM_NORM_EPS = 1e-5
L2_EPS = 1e-12

LANES = 128
SUBLANES = 8
VMEM_LIMIT = 56 * 1024 * 1024
ROW_SPLIT = 2

RWKV_CHUNK = 64
RWKV_GROUP_HEADS = 2
RWKV_PREP_UNROLL = 4
RWKV_LOWRANK_PAD = 256


def _bdot(a, b):
    return jnp.dot(a.astype(BF16), b.astype(BF16), preferred_element_type=F32)


def _bdot_nt(a, b):
    return lax.dot_general(a.astype(BF16), b.astype(BF16), (((1,), (1,)), ((), ())),
                           preferred_element_type=F32)


def _bdot_tn(a, b):
    return lax.dot_general(a.astype(BF16), b.astype(BF16), (((0,), (0,)), ((), ())),
                           preferred_element_type=F32)


def _split_bf16(x, parts):
    out = []
    rem = x
    for _ in range(parts):
        p = rem.astype(BF16)
        out.append(p)
        rem = rem - p.astype(F32)
    return out


def _dot_exact_rhs(x, m_bf16, parts):
    acc = None
    for p in _split_bf16(x, parts):
        t = jnp.dot(p, m_bf16, preferred_element_type=F32)
        acc = t if acc is None else acc + t
    return acc


def _exact_lhs_dot(m_bf16, x, parts):
    acc = None
    for p in _split_bf16(x, parts):
        t = jnp.dot(m_bf16, p, preferred_element_type=F32)
        acc = t if acc is None else acc + t
    return acc


def _rms_mod(x, g, shift, scale):
    y = x * lax.rsqrt(jnp.mean(x * x, axis=-1, keepdims=True) + RMS_EPS) * g
    return y * (1.0 + scale) + shift


def _softplus(z):
    return jnp.maximum(z, 0.0) + jnp.log1p(jnp.exp(-jnp.abs(z)))


def _ada_kernel(c_ref, w_ref, b_ref, o_ref):
    c = c_ref[...]
    cond = c * jax.nn.sigmoid(c)
    o_ref[0] = _bdot(cond, w_ref[0]) + b_ref[0]


def _ada_call(c_pad, ada_w, ada_b):
    depth, dm, n9 = ada_w.shape
    tn = n9 // 4
    rows = c_pad.shape[0]
    return pl.pallas_call(
        _ada_kernel,
        grid=(depth, n9 // tn),
        in_specs=[
            pl.BlockSpec((rows, dm), lambda l, j: (0, 0)),
            pl.BlockSpec((1, dm, tn), lambda l, j: (l, 0, j)),
            pl.BlockSpec((1, 1, tn), lambda l, j: (l, 0, j)),
        ],
        out_specs=pl.BlockSpec((1, rows, tn), lambda l, j: (l, 0, j)),
        out_shape=jax.ShapeDtypeStruct((depth, rows, n9), F32),
        compiler_params=pltpu.CompilerParams(
            dimension_semantics=("arbitrary", "arbitrary"), vmem_limit_bytes=VMEM_LIMIT),
        name="ada_mod",
    )(c_pad, ada_w, ada_b.reshape(depth, 1, n9))


def _row_parts(ref, parts):
    rows = ref.shape[0] // parts
    return [ref[k * rows:(k + 1) * rows, :] for k in range(parts)]


def _store_row_parts(ref, vals):
    rows = ref.shape[0] // len(vals)
    for k, v in enumerate(vals):
        ref[k * rows:(k + 1) * rows, :] = v


def _ffn_half_step(xs, mod_ref, g_ref, wi_ref, wo_ref, fg_ref, *, mod_row, final_norm):
    dff = wo_ref.shape[0]
    shift, scale = mod_ref[0, mod_row:mod_row + 1, :], mod_ref[0, mod_row + 1:mod_row + 2, :]
    hs = [_rms_mod(x, g_ref[...], shift, scale).astype(BF16) for x in xs]
    gus = [jnp.dot(h, wi_ref[...], preferred_element_type=F32) for h in hs]
    acts = [((gu[:, :dff] * jax.nn.sigmoid(gu[:, :dff])) * gu[:, dff:]).astype(BF16) for gu in gus]
    ys = [jnp.dot(a, wo_ref[...], preferred_element_type=F32) for a in acts]
    outs = [x + (0.5 * (1.0 + mod_ref[0, mod_row + 2:mod_row + 3, :])) * y for x, y in zip(xs, ys)]
    if final_norm:
        outs = [o * lax.rsqrt(jnp.mean(o * o, axis=-1, keepdims=True) + RMS_EPS) * fg_ref[...] for o in outs]
    return outs


def _ffn_kernel(x_ref, mod_ref, g_ref, wi_ref, wo_ref, fg_ref, o_ref, *, mod_row, final_norm):
    outs = _ffn_half_step(_row_parts(x_ref, ROW_SPLIT), mod_ref, g_ref, wi_ref, wo_ref, fg_ref,
                          mod_row=mod_row, final_norm=final_norm)
    _store_row_parts(o_ref, outs)


def _resident(shape):
    return pl.BlockSpec(shape, lambda *_: (0,) * len(shape), pipeline_mode=pl.Buffered(1))


def _resident_at(arr, lead):
    tail = arr.shape[len(lead):]
    return pl.BlockSpec((None,) * len(lead) + tail, lambda *_: tuple(lead) + (0,) * len(tail),
                        pipeline_mode=pl.Buffered(1))


def _ffn_call(x2, mod_l, norm_g, w_in, w_out, final_g, *, lead, seq, mod_row, final_norm, tm, regroup=None):
    n, dm = x2.shape
    steps = n // tm
    kern = functools.partial(_ffn_kernel, mod_row=mod_row, final_norm=final_norm)
    in_specs = [
        pl.BlockSpec((tm, dm), lambda i: (i, 0)),
        pl.BlockSpec((1, 9, dm), lambda i: ((i * tm) // seq, 0, 0)),
        _resident((1, dm)),
        _resident_at(w_in, lead),
        _resident_at(w_out, lead),
        _resident((1, dm)),
    ]
    out_specs = [pl.BlockSpec((tm, dm), lambda i: (i, 0))]
    out_shape = [jax.ShapeDtypeStruct((n, dm), F32)]
    extra = []
    if regroup is not None:
        src, segments, out_rows = regroup
        depth, in_rows, width = src.shape
        nblk = width // LANES
        assert width % LANES == 0 and depth * nblk <= steps and out_rows % BF16_SUBLANES == 0
        for l in range(depth):
            col = lambda i, l=l: jnp.clip(i - l * nblk, 0, nblk - 1)
            in_specs.append(pl.BlockSpec((None, in_rows, LANES), lambda i, l=l, col=col: (l, 0, col(i))))
            out_specs.append(pl.BlockSpec((out_rows, LANES), lambda i, col=col: (0, col(i))))
            out_shape.append(jax.ShapeDtypeStruct((out_rows, width), BF16))
            extra.append(src)
        kern = _with_row_regroup(kern, 6, 1, depth, segments, nblk)
    outs = pl.pallas_call(
        kern,
        grid=(steps,),
        in_specs=in_specs,
        out_specs=out_specs,
        out_shape=out_shape,
        compiler_params=pltpu.CompilerParams(
            dimension_semantics=("arbitrary" if regroup is not None else "parallel",),
            vmem_limit_bytes=VMEM_LIMIT),
        name="ffn",
    )(x2, mod_l, norm_g, w_in, w_out, final_g, *extra)
    return outs[0], list(outs[1:])


def _mixin_kernel(x_ref, mod_ref, g_ref, w_ref, mu_ref, cw_ref, cb_ref,
                  xs_ref, qk_ref, pm_ref, pg_ref, prev_ref, xbuf, *, tiles_per_seq):
    tm = x_ref.shape[0]
    KC = cw_ref.shape[0]
    first = (pl.program_id(0) % tiles_per_seq) == 0
    c1 = qk_ref.shape[1]
    c2 = c1 + xs_ref.shape[1]
    c3 = c2 + pm_ref.shape[1]
    c4 = c3 + pg_ref.shape[1]

    @pl.when(first)
    def _():
        prev_ref[...] = jnp.zeros_like(prev_ref)
        xbuf[0:SUBLANES, :] = jnp.zeros((SUBLANES, xbuf.shape[1]), F32)

    @pl.when(jnp.logical_not(first))
    def _():
        xbuf[0:SUBLANES, :] = xbuf[tm:tm + SUBLANES, :]

    h = _rms_mod(x_ref[...], g_ref[...], mod_ref[0, 3:4, :], mod_ref[0, 4:5, :]).astype(BF16)

    def proj(lo, hi):
        return lax.dot_general(h, w_ref[lo:hi, :], (((1,), (1,)), ((), ())), preferred_element_type=F32)

    pr = proj(c1, c2)
    row = lax.broadcasted_iota(jnp.int32, (tm, 1), 0)
    prev = jnp.where(row == 0, prev_ref[...], pltpu.roll(pr, 1, axis=0))
    prev_ref[...] = pr[tm - 1:tm, :]
    xs_ref[...] = pr + (prev - pr) * mu_ref[...]

    xbuf[SUBLANES:SUBLANES + tm, :] = proj(0, c1)
    acc = cb_ref[...] + cw_ref[KC - 1:KC, :] * xbuf[SUBLANES:SUBLANES + tm, :]
    for j in range(KC - 1):
        off = SUBLANES - (KC - 1) + j
        acc = acc + cw_ref[j:j + 1, :] * xbuf[off:off + tm, :]
    qk_ref[...] = acc * jax.nn.sigmoid(acc)

    pm_ref[...] = proj(c2, c3)
    pg_ref[...] = proj(c3, c4)


def _mixin_call(x2, mod_l, norm_g, w_all, widths, mu, conv_w, conv_b, *, seq, tm):
    n, dm = x2.shape
    assert seq % tm == 0 and conv_w.shape[0] - 1 <= SUBLANES and w_all.shape == (sum(widths), dm)
    kern = functools.partial(_mixin_kernel, tiles_per_seq=seq // tm)
    return pl.pallas_call(
        kern,
        grid=(n // tm,),
        in_specs=[
            pl.BlockSpec((tm, dm), lambda i: (i, 0)),
            pl.BlockSpec((1, 9, dm), lambda i: ((i * tm) // seq, 0, 0)),
            _resident((1, dm)),
            _resident(w_all.shape),
            _resident(mu.shape), _resident(conv_w.shape), _resident(conv_b.shape),
        ],
        out_specs=[pl.BlockSpec((tm, w), lambda i: (i, 0)) for w in widths],
        out_shape=[jax.ShapeDtypeStruct((n, w), F32) for w in widths],
        scratch_shapes=[pltpu.VMEM((1, widths[0]), F32), pltpu.VMEM((tm + SUBLANES, widths[1]), F32)],
        compiler_params=pltpu.CompilerParams(
            dimension_semantics=("arbitrary",), vmem_limit_bytes=VMEM_LIMIT),
        name="mix_in",
    )(x2, mod_l, norm_g, w_all, mu, conv_w, conv_b)


def _rwkv_kernel(xs_ref, w0_ref, wup_ref, a0_ref, aup_ref, gup_ref, kk_ref, ka_ref, rk_ref,
                 lng_ref, lnb_ref, o_ref,
                 s_ref, r_s, y_s, bonus_s, g_s, mx_s, n0_s, pc_s, side_work=iter(())):
    tc = xs_ref.shape[0]
    width = o_ref.shape[1]
    C = RWKV_CHUNK
    HD = RWKV_HEAD_DIM
    HG = RWKV_GROUP_HEADS
    GL = HG * HD
    n_groups = width // GL
    shift = HD.bit_length() - 1
    i = pl.program_id(1)

    @pl.when(i == 0)
    def _():
        s_ref[...] = jnp.zeros_like(s_ref)

    rg = lax.broadcasted_iota(jnp.int32, (GL, GL), 0)
    cg = lax.broadcasted_iota(jnp.int32, (GL, GL), 1)
    blockmask = (rg >> shift) == (cg >> shift)
    eye = rg == cg
    ones_bd = jnp.where(blockmask, 1.0, 0.0).astype(BF16)

    def segsum(x):
        return jnp.concatenate(
            [_dot_exact_rhs(x[:, g * GL:(g + 1) * GL], ones_bd, 2) for g in range(n_groups)], axis=1)

    def token_quantities(rows):
        r = xs_ref[rows, 0:width]
        k = xs_ref[rows, width:2 * width]
        v = xs_ref[rows, 2 * width:3 * width]
        low = xs_ref[rows, 3 * width:3 * width + RWKV_LOWRANK_PAD]
        w = w0_ref[...] + _bdot(jnp.tanh(low), wup_ref[...])
        ld = (-math.exp(-0.5)) * jax.nn.sigmoid(w)
        asig = jax.nn.sigmoid(a0_ref[...] + _bdot(low, aup_ref[...]))
        g_s[rows, :] = _bdot(jax.nn.sigmoid(low), gup_ref[...])
        kk = k * kk_ref[...]
        kk = kk * lax.rsqrt(jnp.maximum(segsum(kk * kk), L2_EPS * L2_EPS))
        k2 = k * (1.0 + (asig - 1.0) * ka_ref[...])
        bonus_s[rows, :] = segsum(r * k2 * rk_ref[...]) * v
        return dict(r=r, ld=ld, k=k2, v=v, a=-kk, b=kk * asig)

    lane = lax.broadcasted_iota(jnp.int32, (C, GL), 1)
    tok = lax.broadcasted_iota(jnp.int32, (C, GL), 0)
    strict = (lane & (HD - 1)) < tok
    incl = (lane & (HD - 1)) <= tok
    headmask = [(lane >> shift) == h for h in range(HG)]
    tri = (lax.broadcasted_iota(jnp.int32, (C, C), 1) <= lax.broadcasted_iota(jnp.int32, (C, C), 0))
    tri = jnp.where(tri, 1.0, 0.0).astype(BF16)

    def bd(x):
        return jnp.where(blockmask, jnp.concatenate([x] * HG, axis=0), 0.0).astype(BF16)

    levels = C.bit_length() - 1

    def prepare_stages(c_first):
        inst = []
        tq = token_quantities(slice(c_first * C, (c_first + RWKV_PREP_UNROLL) * C))
        yield
        for u in range(RWKV_PREP_UNROLL):
            c = c_first + u
            sl = slice(c * C, (c + 1) * C)
            loc = slice(u * C, (u + 1) * C)
            ld = tq['ld'][loc]
            lp = _exact_lhs_dot(tri, ld, 3)
            lpc = lp[C - 1:C, :]
            e_lp = jnp.exp(lp)
            e_nlp = jnp.exp(-lp)
            e_end = jnp.exp(lpc - lp)
            rt = tq['r'][loc] * e_lp
            at = tq['a'][loc] * jnp.exp(lp - ld)
            bb = tq['b'][loc]
            kc = tq['k'][loc]
            bt = bb * e_nlp
            kt = kc * e_nlp
            bh = bb * e_end
            kh = kc * e_end
            vv = tq['v'][loc]
            pc = jnp.exp(lpc)
            for g in range(n_groups):
                gs = slice(g * GL, (g + 1) * GL)
                inst.append(dict(sl=sl, gs=gs, idx=c * n_groups + g, at=at[:, gs], rt=rt[:, gs], v=vv[:, gs],
                                 bt=bt[:, gs], kt=kt[:, gs], bh=bh[:, gs], kh=kh[:, gs], pc=pc[:, gs]))
        for t in inst:
            lhs = jnp.concatenate([t['at'], t['rt']], axis=0)
            rows = [jnp.where(headmask[h], t['bt'], 0.0) for h in range(HG)]
            rows += [jnp.where(headmask[h], t['kt'], 0.0) for h in range(HG)]
            gram = _bdot_nt(lhs, jnp.concatenate(rows, axis=0))
            t['a_ab'] = jnp.where(strict, gram[:C, :GL], 0.0)
            t['a_ak'] = jnp.where(strict, gram[:C, GL:], 0.0)
            t['a_rb'] = jnp.where(incl, gram[C:, :GL], 0.0).astype(BF16)
            t['a_rk'] = jnp.where(incl, gram[C:, GL:], 0.0)
        yield
        for t in inst:
            t['bdv'] = bd(t['v'])
            both = _bdot(jnp.concatenate([t['a_ak'], t['a_rk']], axis=0), t['bdv'])
            t['akv'] = both[:C]
            t['arkv'] = both[C:]
        yield
        for t in inst:
            t['tr'] = t['a_ab']
            t['nm'] = jnp.dot(t['a_ab'].astype(BF16), bd(t['a_ab']), preferred_element_type=F32)
        yield
        for lvl in range(1, levels):
            for t in inst:
                if lvl + 1 < levels:
                    both = _bdot(jnp.concatenate([t['tr'], t['nm']], axis=0), bd(t['nm']))
                    t['tr'] = t['tr'] + t['nm'] + both[:C]
                    t['nm'] = both[C:]
                else:
                    t['tr'] = t['tr'] + t['nm'] + _bdot(t['tr'], bd(t['nm']))
            yield
        for t in inst:
            wu = _bdot(t['tr'], jnp.concatenate([bd(t['at']), bd(t['akv'])], axis=1))
            t['w'] = t['at'] + wu[:, :GL]
            t['u0'] = t['akv'] + wu[:, GL:]
        yield
        for t in inst:
            ry = jnp.dot(t['a_rb'], jnp.concatenate([bd(t['w']), bd(t['u0'])], axis=1),
                         preferred_element_type=F32)
            r_s[t['sl'], t['gs']] = t['rt'] + ry[:, :GL]
            y_s[t['sl'], t['gs']] = ry[:, GL:] + t['arkv']
        yield
        for t in inst:
            mn = _bdot_tn(t['bh'], jnp.concatenate([t['w'], t['u0']], axis=1))
            mx_s[t['idx']] = jnp.where(blockmask, mn[:, :GL], 0.0).astype(BF16)
            n0_s[t['idx']] = jnp.where(blockmask, mn[:, GL:] + _bdot_tn(t['kh'], t['v']), 0.0)
            pc_s[t['idx']] = jnp.broadcast_to(
                jnp.sum(jnp.where(eye, jnp.broadcast_to(t['pc'], (GL, GL)), 0.0), axis=1, keepdims=True),
                (GL, GL))

    def advance(c):
        sl = slice(c * C, (c + 1) * C)
        for g in range(n_groups):
            gs = slice(g * GL, (g + 1) * GL)
            idx = c * n_groups + g
            s_old = s_ref[g]
            s_bf = s_old.astype(BF16)
            y_s[sl, gs] = y_s[sl, gs] + jnp.dot(r_s[sl, gs].astype(BF16), s_bf, preferred_element_type=F32)
            s_ref[g] = pc_s[idx] * s_old + n0_s[idx] + jnp.dot(mx_s[idx], s_bf, preferred_element_type=F32)

    def finish(rows):
        y = y_s[rows, :]
        inv = 1.0 / HD
        yc = y - segsum(y) * inv
        var = segsum(yc * yc) * inv
        yn = yc * lax.rsqrt(var + RWKV_GN_EPS) * lng_ref[...] + lnb_ref[...]
        o_ref[rows, :] = (yn + bonus_s[rows, :]) * g_s[rows, :]

    n_chunks = tc // C
    pending = []
    done_rows = None
    for c_first in range(0, n_chunks, RWKV_PREP_UNROLL):
        stages = prepare_stages(c_first)
        for k, _ in enumerate(stages):
            if pending and k % 2 == 0:
                advance(pending.pop(0))
            elif not pending and done_rows is not None:
                finish(done_rows)
                done_rows = None
            next(side_work, None)
        for c in pending:
            advance(c)
        if done_rows is not None:
            finish(done_rows)
        pending = list(range(c_first, c_first + RWKV_PREP_UNROLL))
        done_rows = slice(c_first * C, (c_first + RWKV_PREP_UNROLL) * C)
    for c in pending:
        advance(c)
    finish(done_rows)
    for _ in side_work:
        pass


BF16_SUBLANES = 16


def _cast_blocking(rows, cols, steps):
    for cb in range(1, steps + 1):
        rb = steps // cb
        if (steps % cb == 0 and rows % rb == 0 and cols % cb == 0
                and (rows // rb) % BF16_SUBLANES == 0 and (cols // cb) % LANES == 0):
            return rb, cb
    raise ValueError((rows, cols, steps))


def _cast_rider_specs(casts, nt):
    in_specs, out_specs, out_shapes = [], [], []
    for arr, lead, (rb, cb) in casts:
        rows, cols = arr.shape[-2:]
        blk = (rows // rb, cols // cb)
        in_specs.append(pl.BlockSpec(
            (None,) * len(lead) + blk,
            lambda b, i, lead=lead, cb=cb: tuple(lead) + ((b * nt + i) // cb, (b * nt + i) % cb)))
        out_specs.append(pl.BlockSpec(blk, lambda b, i, cb=cb: ((b * nt + i) // cb, (b * nt + i) % cb)))
        out_shapes.append(jax.ShapeDtypeStruct((rows, cols), BF16))
    return in_specs, out_specs, out_shapes


def _cast_block(src, dst):
    dst[...] = src[...].astype(dst.dtype)


def _with_cast_riders(body, n_in, n_out, n_cast, fn=_cast_block):
    def kern(*refs):
        ins = refs[:n_in]
        cast_ins = refs[n_in:n_in + n_cast]
        outs = refs[n_in + n_cast:n_in + n_cast + n_out]
        cast_outs = refs[n_in + n_cast + n_out:n_in + 2 * n_cast + n_out]
        scratch = refs[n_in + 2 * n_cast + n_out:]
        for src, dst in zip(cast_ins, cast_outs):
            fn(src, dst)
        body(*ins, *outs, *scratch)
    return kern


def _regroup_rows(segments, src, dst):
    dst[...] = jnp.zeros(dst.shape, dst.dtype)
    for s, d, w in segments:
        dst[d:d + w, :] = src[s:s + w, :].astype(dst.dtype)


def _with_row_regroup(body, n_in, n_out, depth, segments, nblk):
    def kern(*refs):
        ins = refs[:n_in]
        srcs = refs[n_in:n_in + depth]
        outs = refs[n_in + depth:n_in + depth + n_out]
        dsts = refs[n_in + depth + n_out:n_in + 2 * depth + n_out]
        scratch = refs[n_in + 2 * depth + n_out:]
        step = pl.program_id(0)
        for l, (src, dst) in enumerate(zip(srcs, dsts)):
            @pl.when(jnp.logical_and(step >= l * nblk, step < (l + 1) * nblk))
            def _(src=src, dst=dst):
                _regroup_rows(segments, src, dst)
        body(*ins, *outs, *scratch)
    return kern


def _rwkv_specs(ncols, *, seq, tc):
    width = RWKV_HEADS * RWKV_HEAD_DIM
    gl = RWKV_GROUP_HEADS * RWKV_HEAD_DIM
    nt = seq // tc
    assert seq % tc == 0 and tc % (RWKV_CHUNK * RWKV_PREP_UNROLL) == 0 and width % gl == 0
    nblk = (tc // RWKV_CHUNK) * (width // gl)
    vec = _resident((1, width))
    lowmat = _resident((RWKV_LOWRANK_PAD, width))
    tile = pltpu.VMEM((tc, width), F32)
    in_specs = [pl.BlockSpec((tc, ncols), lambda b, i: (b * nt + i, 0)),
                vec, lowmat, vec, lowmat, lowmat, vec, vec, vec, vec, vec]
    out_spec = pl.BlockSpec((tc, width), lambda b, i: (b * nt + i, 0))
    scratch = [
        pltpu.VMEM((width // gl, gl, gl), F32),
        tile, tile, tile, tile,
        pltpu.VMEM((nblk, gl, gl), BF16),
        pltpu.VMEM((nblk, gl, gl), F32),
        pltpu.VMEM((nblk, gl, gl), F32),
    ]
    return in_specs, out_spec, scratch


def _cummax_rows(x):
    n = x.shape[0]
    row = lax.broadcasted_iota(jnp.int32, x.shape, 0)
    s = 1
    while s < n:
        x = jnp.maximum(x, jnp.where(row >= s, pltpu.roll(x, s, axis=0), -jnp.inf))
        s *= 2
    return x


def _mlstm_stages(qk_s, p_ref, gbi_ref, gbf_ref, lng_ref, o_ref, cn_ref, m_ref):
    tcm = p_ref.shape[0]
    NH, DH, L = MLSTM_HEADS, MLSTM_HEAD_DIM, MLSTM_CHUNK
    wd = NH * DH
    i = pl.program_id(1)

    @pl.when(i == 0)
    def _():
        cn_ref[...] = jnp.zeros_like(cn_ref)
        m_ref[...] = jnp.full(m_ref.shape, -jnp.inf, F32)

    rr = lax.broadcasted_iota(jnp.int32, (L, L), 0)
    cc = lax.broadcasted_iota(jnp.int32, (L, L), 1)
    causal = cc <= rr
    tri = jnp.where(causal, 1.0, 0.0).astype(BF16)
    eye_l = jnp.where(cc == rr, 1.0, 0.0).astype(BF16)
    eye_h = eye_l[0:2 * SUBLANES, :]
    ones_v = jnp.ones((L, DH), BF16)
    scale = DH ** -0.5
    heads = range(NH)

    def chunk(c):
        sl = slice(c * L, (c + 1) * L)
        li = p_ref[sl, 2 * wd:2 * wd + LANES] + gbi_ref[...]
        lf = -_softplus(-(p_ref[sl, 2 * wd + LANES:2 * wd + 2 * LANES] + gbf_ref[...]))
        b_col = _exact_lhs_dot(tri, lf, 3)
        q = [(qk_s[sl, h * DH:(h + 1) * DH] * scale).astype(BF16) for h in heads]
        k = [qk_s[sl, wd + h * DH:wd + (h + 1) * DH].astype(BF16) for h in heads]
        s = [lax.dot_general(q[h], k[h], (((1,), (1,)), ((), ())), preferred_element_type=F32) for h in heads]
        yield
        xcol = li - b_col
        m_prev = m_ref[...]
        zcol = jnp.maximum(m_prev, _cummax_rows(xcol))
        b_end = b_col[L - 1:L, :]
        m_new = jnp.maximum(b_end + m_prev, jnp.max(b_end + xcol, axis=0, keepdims=True))
        keep = jnp.exp(b_end + m_prev - m_new)
        m_ref[...] = m_new
        wshift = b_end - m_new
        xrow = None
        for part in _split_bf16(xcol, 3):
            t = lax.dot_general(eye_h, part, (((1,), (1,)), ((), ())), preferred_element_type=F32)
            xrow = t if xrow is None else xrow + t
        vx = [jnp.concatenate([p_ref[sl, h * DH:(h + 1) * DH].astype(BF16), ones_v], axis=1)
              for h in heads]
        cn = [cn_ref[h] for h in heads]
        kt = [lax.dot_general(eye_l, k[h], (((1,), (1,)), ((), ())), preferred_element_type=F32) for h in heads]
        qc = [jnp.dot(q[h], cn[h].astype(BF16), preferred_element_type=F32) for h in heads]
        yield
        zb = [jnp.broadcast_to(zcol[:, h:h + 1], (L, DH)) for h in heads]
        bb = [jnp.broadcast_to(b_col[:, h:h + 1], (L, DH)) for h in heads]
        pw = [(s[h] * jnp.exp(jnp.where(causal, xrow[h:h + 1, :] - zb[h], -jnp.inf))).astype(BF16) for h in heads]
        nd = [jnp.dot(pw[h], vx[h], preferred_element_type=F32) for h in heads]
        for h in heads:
            wrow = jnp.exp(xrow[h:h + 1, :] + wshift[:, h:h + 1])
            cn_ref[h] = keep[:, h:h + 1] * cn[h] + jnp.dot((kt[h] * wrow).astype(BF16), vx[h],
                                                            preferred_element_type=F32)
        yield
        for h in heads:
            hs = slice(h * DH, (h + 1) * DH)
            inter = jnp.exp(m_prev[:, h:h + 1] - zb[h])
            num = nd[h][:, :DH] + inter * qc[h][:, :DH]
            den = nd[h][:, DH:] + inter * qc[h][:, DH:]
            hh = num / jnp.maximum(jnp.abs(den), jnp.exp(-(bb[h] + zb[h])))
            hc = hh - jnp.mean(hh, axis=1, keepdims=True)
            hn = hc * lax.rsqrt(jnp.mean(hc * hc, axis=1, keepdims=True) + MLSTM_NORM_EPS)
            og = jax.nn.sigmoid(p_ref[sl, wd + h * DH:wd + (h + 1) * DH])
            o_ref[sl, hs] = og * (hn * lng_ref[:, hs])
        yield

    for c in range(tcm // L):
        yield from chunk(c)


def _mlstm_specs(ncols, *, seq, tc):
    nh, dh = MLSTM_HEADS, MLSTM_HEAD_DIM
    wd = nh * dh
    nt = seq // tc
    assert seq % tc == 0 and tc % MLSTM_CHUNK == 0
    in_specs = [
        pl.BlockSpec((tc, 2 * wd), lambda b, i: (b * nt + i, 0)),
        pl.BlockSpec((tc, ncols), lambda b, i: (b * nt + i, 0)),
        _resident((1, LANES)),
        _resident((1, LANES)),
        _resident((1, wd)),
    ]
    out_spec = pl.BlockSpec((tc, wd), lambda b, i: (b * nt + i, 0))
    scratch = [pltpu.VMEM((nh, dh, 2 * dh), F32), pltpu.VMEM((1, LANES), F32)]
    return in_specs, out_spec, scratch


N_RWKV_IN, N_RWKV_SCRATCH, N_MLSTM_IN = 11, 8, 5


def _mixers_kernel(*refs):
    r_in = refs[:N_RWKV_IN]
    m_in = refs[N_RWKV_IN:N_RWKV_IN + N_MLSTM_IN]
    yr_ref, ym_ref = refs[N_RWKV_IN + N_MLSTM_IN:N_RWKV_IN + N_MLSTM_IN + 2]
    scratch = refs[N_RWKV_IN + N_MLSTM_IN + 2:]
    _rwkv_kernel(*r_in, yr_ref, *scratch[:N_RWKV_SCRATCH],
                 side_work=_mlstm_stages(*m_in, ym_ref, *scratch[N_RWKV_SCRATCH:]))


def _mixers_call(rwkv_args, mlstm_args, *, batch, seq, tc, casts=()):
    n = rwkv_args[0].shape[0]
    nt = seq // tc
    r_in, r_out, r_scratch = _rwkv_specs(rwkv_args[0].shape[1], seq=seq, tc=tc)
    m_in, m_out, m_scratch = _mlstm_specs(mlstm_args[1].shape[1], seq=seq, tc=tc)
    assert (len(r_in), len(r_scratch), len(m_in)) == (N_RWKV_IN, N_RWKV_SCRATCH, N_MLSTM_IN)
    casts = [(arr, lead, _cast_blocking(*arr.shape[-2:], batch * nt)) for arr, lead in casts]
    cast_in, cast_out, cast_shapes = _cast_rider_specs(casts, nt)
    outs = pl.pallas_call(
        _with_cast_riders(_mixers_kernel, len(r_in) + len(m_in), 2, len(casts)),
        grid=(batch, nt),
        in_specs=r_in + m_in + cast_in,
        out_specs=[r_out, m_out] + cast_out,
        out_shape=[jax.ShapeDtypeStruct((n, RWKV_HEADS * RWKV_HEAD_DIM), F32),
                   jax.ShapeDtypeStruct((n, MLSTM_HEADS * MLSTM_HEAD_DIM), F32)] + cast_shapes,
        scratch_shapes=r_scratch + m_scratch,
        compiler_params=pltpu.CompilerParams(
            dimension_semantics=("parallel", "arbitrary"), vmem_limit_bytes=VMEM_LIMIT),
        name="mixers",
    )(*rwkv_args, *mlstm_args, *[c[0] for c in casts])
    return outs[0], outs[1], list(outs[2:])


def _mixout_ffn_kernel(x_ref, yr_ref, ym_ref, pg_ref, mod_ref, wr_ref, wm_ref, wo_ref,
                       g_ref, wi_ref, wf_ref, fg_ref, o_ref, *, final_norm):
    dm = x_ref.shape[1]
    zrs = [jnp.dot(y.astype(BF16), wr_ref[...], preferred_element_type=F32) for y in _row_parts(yr_ref, ROW_SPLIT)]
    zms = [jnp.dot(y.astype(BF16), wm_ref[...], preferred_element_type=F32) for y in _row_parts(ym_ref, ROW_SPLIT)]
    mixes = [(jax.nn.sigmoid(pg[:, 0:dm]) * zr + jax.nn.sigmoid(pg[:, dm:2 * dm]) * zm).astype(BF16)
             for pg, zr, zm in zip(_row_parts(pg_ref, ROW_SPLIT), zrs, zms)]
    outs = [jnp.dot(m, wo_ref[...], preferred_element_type=F32) for m in mixes]
    xs = [x + (1.0 + mod_ref[0, 5:6, :]) * o for x, o in zip(_row_parts(x_ref, ROW_SPLIT), outs)]
    _store_row_parts(o_ref, _ffn_half_step(xs, mod_ref, g_ref, wi_ref, wf_ref, fg_ref, mod_row=6,
                                           final_norm=final_norm))


def _mixout_ffn_call(x2, y_r, y_m, p_g, mod_l, w_r, w_m, w_o, norm_g, w_in, w_out, final_g, *,
                     layer, seq, final_norm, tm):
    n, dm = x2.shape
    rowblk = lambda cols: pl.BlockSpec((tm, cols), lambda i: (i, 0))
    kern = functools.partial(_mixout_ffn_kernel, final_norm=final_norm)
    return pl.pallas_call(
        kern,
        grid=(n // tm,),
        in_specs=[
            rowblk(dm), rowblk(y_r.shape[1]), rowblk(y_m.shape[1]), rowblk(p_g.shape[1]),
            pl.BlockSpec((1, 9, dm), lambda i: ((i * tm) // seq, 0, 0)),
            _resident_at(w_r, (layer,)), _resident_at(w_m, (layer,)), _resident_at(w_o, (layer,)),
            _resident((1, dm)), _resident_at(w_in, ()), _resident_at(w_out, ()),
            _resident((1, dm)),
        ],
        out_specs=rowblk(dm),
        out_shape=jax.ShapeDtypeStruct((n, dm), F32),
        compiler_params=pltpu.CompilerParams(
            dimension_semantics=("parallel",), vmem_limit_bytes=VMEM_LIMIT),
        name="mix_out_ffn",
    )(x2, y_r, y_m, p_g, mod_l, w_r, w_m, w_o, norm_g, w_in, w_out, final_g)


def _pad_cols(a, n):
    return jnp.pad(a, ((0, 0), (0, n - a.shape[1])))


def _row(a, n=None):
    a = a.reshape(1, -1)
    return a if n is None else _pad_cols(a, n)


def _lowrank_pad(mat, row0):
    out = jnp.zeros((RWKV_LOWRANK_PAD, mat.shape[1]), F32)
    return lax.dynamic_update_slice(out, mat, (row0, 0)).astype(BF16)


def kernel(x, c, ada_w, ada_b, norm_g, ffn_w_in, ffn_w_out, mix_w_in, rwkv_mu, rwkv_w0, rwkv_w_up, rwkv_a0, rwkv_a_up, rwkv_g_up, rwkv_k_k, rwkv_k_a, rwkv_r_k, rwkv_ln_g, rwkv_ln_b, mlstm_conv_w, mlstm_conv_b, mlstm_gate_b, mlstm_ln_g, branch_w_rwkv, branch_w_mlstm, mix_w_out, final_g):
    batch, seq, dm = x.shape
    depth = ada_w.shape[0]
    n = batch * seq
    rw = RWKV_HEADS * RWKV_HEAD_DIM
    mw = MLSTM_HEADS * MLSTM_HEAD_DIM
    rwkv_cols = 3 * rw + RWKV_DECAY_RANK + RWKV_A_RANK + RWKV_GATE_RANK
    mlstm_cols = 4 * mw + 2 * MLSTM_HEADS
    rwkv_pad = 3 * rw + RWKV_LOWRANK_PAD

    tm_ffn = min(512, seq)
    tm_mix = min(512, seq)
    tc_mixers = min(512, seq)

    c_pad = jnp.pad(c, ((0, SUBLANES - batch % SUBLANES if batch % SUBLANES else 0), (0, 0)))
    mod = _ada_call(c_pad, ada_w, ada_b)[:, :batch].reshape(depth, batch, 9, dm)

    mix_w_in_t = jnp.swapaxes(mix_w_in, 1, 2)
    v0 = rwkv_cols + 2 * mw
    gate0 = rwkv_cols + 4 * mw
    d_r = 2 * mw
    d_m = d_r + rwkv_pad
    d_g = d_m + 2 * mw + 2 * LANES
    mix_segments = [
        (rwkv_cols, 0, 2 * mw),
        (0, d_r, rwkv_cols),
        (v0, d_m, 2 * mw),
        (gate0, d_m + 2 * mw, MLSTM_HEADS),
        (gate0 + MLSTM_HEADS, d_m + 2 * mw + LANES, MLSTM_HEADS),
        (rwkv_cols + mlstm_cols, d_g, 2 * dm)]
    mix_widths = [rwkv_pad, 2 * mw, 2 * mw + 2 * LANES, 2 * dm]

    branch_r_bf = branch_w_rwkv.astype(BF16)
    branch_m_bf = branch_w_mlstm.astype(BF16)
    mix_w_out_bf = mix_w_out.astype(BF16)
    final_row = _row(final_g)

    ffn_in_bf = {(0, 0): ffn_w_in[0, 0].astype(BF16)}
    ffn_out_bf = {(0, 0): ffn_w_out[0, 0].astype(BF16)}

    x2 = x.reshape(n, dm)
    for l in range(depth):
        mod_l = mod[l]
        x2, regrouped = _ffn_call(
            x2, mod_l, _row(norm_g[l, 0]), ffn_in_bf[(l, 0)], ffn_out_bf[(l, 0)], final_row,
            lead=(), seq=seq, mod_row=0, final_norm=False, tm=tm_ffn,
            regroup=(mix_w_in_t, mix_segments, sum(mix_widths)) if l == 0 else None)
        if l == 0:
            mix_w_all = regrouped
        later = [(l, 1)] + ([(l + 1, 0)] if l + 1 < depth else [])

        xs_r, qk_m, p_m, p_g = _mixin_call(
            x2, mod_l, _row(norm_g[l, 1]), mix_w_all[l], mix_widths, _row(rwkv_mu[l], rwkv_pad),
            mlstm_conv_w[l], _row(mlstm_conv_b[l]), seq=seq, tm=tm_mix)

        rwkv_args = [
            xs_r, _row(rwkv_w0[l]),
            _lowrank_pad(rwkv_w_up[l], 0), _row(rwkv_a0[l]),
            _lowrank_pad(rwkv_a_up[l], RWKV_DECAY_RANK),
            _lowrank_pad(rwkv_g_up[l], RWKV_DECAY_RANK + RWKV_A_RANK),
            _row(rwkv_k_k[l]), _row(rwkv_k_a[l]), _row(rwkv_r_k[l]), _row(rwkv_ln_g[l]), _row(rwkv_ln_b[l])]
        mlstm_args = [
            qk_m, p_m, _row(mlstm_gate_b[l, :MLSTM_HEADS], LANES), _row(mlstm_gate_b[l, MLSTM_HEADS:], LANES),
            _row(mlstm_ln_g[l])]
        y_r, y_m, cast = _mixers_call(
            rwkv_args, mlstm_args, batch=batch, seq=seq, tc=tc_mixers,
            casts=[(ffn_w_in, lf) for lf in later] + [(ffn_w_out, lf) for lf in later])
        ffn_in_bf.update(zip(later, cast[:len(later)]))
        ffn_out_bf.update(zip(later, cast[len(later):]))

        x2 = _mixout_ffn_call(
            x2, y_r, y_m, p_g, mod_l, branch_r_bf, branch_m_bf, mix_w_out_bf, _row(norm_g[l, 2]),
            ffn_in_bf[(l, 1)], ffn_out_bf[(l, 1)], final_row, layer=l, seq=seq, final_norm=(l == depth - 1),
            tm=tm_ffn)
    return x2.reshape(batch, seq, dm)
```
